```python
import jax, jax.numpy as jnp
from jax import lax
import numpy as np

D_MODEL = 2048
BATCH = 16
SEQ = 2048
DEPTH = 4
DEC_BATCH = 2
DEC_SEQ = 8192
PAST_LEN = 128

HEAD_DIM = 128
A_HEADS = 6
B_HEADS = 4
C_HEADS = 6
A_WIDTH = A_HEADS * HEAD_DIM
B_WIDTH = B_HEADS * HEAD_DIM
C_WIDTH = C_HEADS * HEAD_DIM
MIX_WIDTH = A_WIDTH + B_WIDTH + C_WIDTH
IN_WIDTH = 5 * A_WIDTH + 3 * B_WIDTH + 3 * C_WIDTH
SPLIT_POINTS = [int(s) for s in np.cumsum([A_WIDTH] * 5 + [B_WIDTH] * 3 + [C_WIDTH] * 2)]
HGRN_CHUNK = 64
GRID_W = 64
NA_WIN_R = 8
NA_WIN_C = 16
NA_QBLK = 16
NA_BAND = 32
DIL_GROUPS = ((128, 1), (512, 4), (2048, 16))
C_HEADS_PER_GROUP = C_HEADS // len(DIL_GROUPS)
DIL_BLK = 64
ROPE_THETA = 10000.0
N_EXPERTS = 16
EC_CAPACITY_FACTOR = 2
EXPERT_FF = 1024
PLE_DIM = 256
EPS = 1e-6

kernel_name = "hybrid_bidir_hgrn2_natten_dilated_ec"


def rms_norm(x, g):
    x32 = x.astype(jnp.float32)
    y = x32 * lax.rsqrt(jnp.mean(x32 * x32, axis=-1, keepdims=True) + EPS) * g.astype(jnp.float32)
    return y.astype(x.dtype)


def rope(x, pos):
    half = x.shape[-1] // 2
    inv = 1.0 / (ROPE_THETA ** (jnp.arange(half, dtype=jnp.float32) * 2.0 / x.shape[-1]))
    ang = pos[:, None] * inv[None, :]
    cos, sin = jnp.cos(ang)[:, None, :], jnp.sin(ang)[:, None, :]
    x1, x2 = x[..., :half], x[..., half:]
    return jnp.concatenate([x1 * cos - x2 * sin, x2 * cos + x1 * sin], axis=-1)


def hgrn2_scan(q, k, v, log_f):
    B, H, T, dk = q.shape
    dv = v.shape[-1]
    n = T // HGRN_CHUNK

    def chunks(a):
        return jnp.moveaxis(a.reshape(B, H, n, HGRN_CHUNK, a.shape[-1]), 2, 0)

    tri = jnp.tril(jnp.ones((HGRN_CHUNK, HGRN_CHUNK), dtype=bool))

    def step(S, inp):
        qc, kc, vc, lf = inp
        b = jnp.cumsum(lf, axis=-2)
        diff = b[:, :, :, None, :] - b[:, :, None, :, :]
        decay = jnp.exp(jnp.where(tri[:, :, None], diff, -jnp.inf))
        scores = jnp.einsum('bhtd,bhsd,bhtsd->bhts', qc, kc, decay)
        o = jnp.einsum('bhts,bhsv->bhtv', scores, vc) + jnp.einsum('bhtd,bhdv->bhtv', qc * jnp.exp(b), S)
        b_last = b[:, :, -1, :]
        S = jnp.exp(b_last)[..., None] * S + jnp.einsum(
            'bhsd,bhsv->bhdv', kc * jnp.exp(b_last[:, :, None, :] - b), vc)
        return S, o

    S0 = jnp.zeros((B, H, dk, dv), jnp.float32)
    _, o = lax.scan(step, S0, (chunks(q), chunks(k), chunks(v), chunks(log_f)))
    return jnp.moveaxis(o, 0, 2).reshape(B, H, T, dv)


def hgrn2_mixer(a_q, a_ff, a_fb, a_i, a_g, lb_f, lb_b, norm_g):
    B, T, _ = a_q.shape

    def heads_t(a):
        return a.astype(jnp.float32).reshape(B, T, A_HEADS, HEAD_DIM).transpose(0, 2, 1, 3)

    q = jax.nn.silu(heads_t(a_q))
    v = heads_t(a_i)

    def gate(f_pre, lb):
        lb = lb.astype(jnp.float32).reshape(A_HEADS, 1, HEAD_DIM)
        f = lb + (1.0 - lb) * jax.nn.sigmoid(heads_t(f_pre))
        return jnp.log(f), 1.0 - f

    lf_f, k_f = gate(a_ff, lb_f)
    lf_b, k_b = gate(a_fb, lb_b)
    fl = lambda a: jnp.flip(a, axis=2)
    o = hgrn2_scan(q, k_f, v, lf_f) + fl(hgrn2_scan(fl(q), fl(k_b), fl(v), fl(lf_b)))
    o = o.transpose(0, 2, 1, 3)
    o = rms_norm(o, norm_g) * jax.nn.silu(a_g.astype(jnp.float32).reshape(B, T, A_HEADS, HEAD_DIM))
    return o.reshape(B, T, A_WIDTH)


def neighborhood_attention(q, k, v, rpb):
    B, T, H, dh = q.shape
    rows = T // GRID_W
    wr = min(NA_WIN_R, rows)
    to_grid = lambda a: a.reshape(B, rows, GRID_W, H, dh).transpose(0, 3, 1, 2, 4)
    qg, kg, vg = to_grid(q), to_grid(k), to_grid(v)
    n_cb = GRID_W // NA_QBLK
    cb = np.arange(n_cb)
    band_start = np.clip(cb * NA_QBLK - NA_WIN_C // 2, 0, GRID_W - NA_BAND)
    band_cols = band_start[:, None] + np.arange(NA_BAND)
    q_cols = cb[:, None] * NA_QBLK + np.arange(NA_QBLK)
    win_start = np.clip(q_cols - NA_WIN_C // 2, 0, GRID_W - NA_WIN_C)
    kc = band_cols[:, None, :]
    col_ok = (kc >= win_start[..., None]) & (kc < win_start[..., None] + NA_WIN_C)
    dc_idx = np.clip(kc - q_cols[..., None] + NA_WIN_C - 1, 0, 2 * NA_WIN_C - 2)
    rpb_c = rpb.astype(jnp.float32)[:, :, dc_idx]
    col_ok = jnp.asarray(col_ok)[:, :, None, :]
    scale = 1.0 / np.sqrt(dh)

    def one_row(r):
        rs = jnp.clip(r - wr // 2, 0, rows - wr)
        k_band = lax.dynamic_slice_in_dim(kg, rs, wr, axis=2)[:, :, :, band_cols]
        v_band = lax.dynamic_slice_in_dim(vg, rs, wr, axis=2)[:, :, :, band_cols]
        q_row = lax.dynamic_index_in_dim(qg, r, axis=2, keepdims=False).reshape(B, H, n_cb, NA_QBLK, dh)
        s = jnp.einsum('bhjqd,bhrjkd->bhjqrk', q_row, k_band) * scale
        dr_idx = rs - r + jnp.arange(wr) + NA_WIN_R - 1
        bias = jnp.take(rpb_c, dr_idx, axis=1).transpose(0, 2, 3, 1, 4)
        s = jnp.where(col_ok, s + bias, -jnp.inf)
        p = jax.nn.softmax(s.reshape(B, H, n_cb, NA_QBLK, wr * NA_BAND), axis=-1).reshape(s.shape)
        o = jnp.einsum('bhjqrk,bhrjkd->bhjqd', p, v_band)
        return o.reshape(B, H, GRID_W, dh)

    out = lax.map(one_row, jnp.arange(rows))
    return out.transpose(1, 0, 3, 2, 4).reshape(B, T, H, dh)


def banded_attention(q, k, v, half):
    L, dh = q.shape[-2], q.shape[-1]
    lead = q.shape[:-2]
    nb = -(-L // DIL_BLK)
    Lp = nb * DIL_BLK
    zp = [(0, 0)] * len(lead)
    qb = jnp.pad(q, zp + [(0, Lp - L), (0, 0)]).reshape(*lead, nb, DIL_BLK, dh)

    def windows(a):
        ap = jnp.pad(a, zp + [(DIL_BLK, Lp - L + DIL_BLK), (0, 0)]).reshape(*lead, nb + 2, DIL_BLK, dh)
        return jnp.concatenate([ap[..., :-2, :, :], ap[..., 1:-1, :, :], ap[..., 2:, :, :]], axis=-2)

    kw, vw = windows(k), windows(v)
    qpos = np.arange(nb)[:, None] * DIL_BLK + np.arange(DIL_BLK)
    kpos = (np.arange(nb)[:, None] - 1) * DIL_BLK + np.arange(3 * DIL_BLK)
    mask = (np.abs(qpos[:, :, None] - kpos[:, None, :]) <= half) & ((kpos >= 0) & (kpos < L))[:, None, :]
    s = jnp.einsum('...nqd,...nkd->...nqk', qb, kw) * (1.0 / np.sqrt(dh))
    s = jnp.where(jnp.asarray(mask), s, -jnp.inf)
    m = lax.stop_gradient(jnp.max(s, axis=-1, keepdims=True))
    p = jnp.exp(s - m)
    den = jnp.sum(p, axis=-1, keepdims=True)
    o = jnp.einsum('...nqk,...nkd->...nqd', p, vw) / den
    lse = (m + jnp.log(den))[..., 0]
    return o.reshape(*lead, Lp, dh)[..., :L, :], lse.reshape(*lead, Lp)[..., :L]


def dilated_attention(q, k, v):
    B, T, _, dh = q.shape
    hp = C_HEADS_PER_GROUP
    outs, lses = [], []
    for g, (w, d) in enumerate(DIL_GROUPS):
        sl = slice(g * hp, (g + 1) * hp)
        dil = lambda a: a[:, :, sl].reshape(B, T // d, d, hp, dh).transpose(0, 2, 3, 1, 4)
        o, lse = banded_attention(dil(q), dil(k), dil(v), w // (2 * d))
        outs.append(o.transpose(0, 3, 1, 2, 4).reshape(B, T, hp, dh))
        lses.append(lse.transpose(0, 3, 1, 2).reshape(B, T, hp))
    alpha = jax.nn.softmax(jnp.stack(lses, axis=0), axis=0)
    return jnp.concatenate([o * alpha[g][..., None] for g, o in enumerate(outs)], axis=2)


def expert_choice_ffn(x, w_router, w_gate, w_up, w_down):
    B, T, D = x.shape
    N = B * T
    xf = x.reshape(N, D)
    aff = jax.nn.softmax((xf @ w_router).astype(jnp.float32), axis=-1)
    cap = (EC_CAPACITY_FACTOR * N) // N_EXPERTS
    gates, idx = lax.top_k(aff.T, cap)
    xe = xf[idx]
    hdn = jax.nn.silu(jnp.einsum('ecd,edf->ecf', xe, w_gate)) * jnp.einsum('ecd,edf->ecf', xe, w_up)
    ye = jnp.einsum('ecf,efd->ecd', hdn, w_down) * gates[..., None].astype(x.dtype)
    out = jnp.zeros((N, D), x.dtype).at[idx.reshape(-1)].add(ye.reshape(-1, D).astype(x.dtype))
    return out.reshape(B, T, D)


def _layer(h, p_i, lb_f, lb_b, norm1_g, w_in, a_norm_g, b_qnorm_g, b_knorm_g, b_rpb,
           c_qnorm_g, c_knorm_g, w_out, norm2_g, w_router, w_gate, w_up, w_down,
           w_ple, ple_norm_g, gate_norm_g, w_ple_gate):
    B, T, _ = h.shape
    f32 = jnp.float32
    z = rms_norm(h, norm1_g) @ w_in
    (a_q, a_ff, a_fb, a_i, a_g, b_q, b_k, b_v, c_q, c_k, c_v) = jnp.split(z, SPLIT_POINTS, axis=-1)
    a_out = hgrn2_mixer(a_q, a_ff, a_fb, a_i, a_g, lb_f, lb_b, a_norm_g)
    heads = lambda a, nh: a.astype(f32).reshape(B, T, nh, HEAD_DIM)
    bq = rms_norm(heads(b_q, B_HEADS), b_qnorm_g)
    bk = rms_norm(heads(b_k, B_HEADS), b_knorm_g)
    b_out = neighborhood_attention(bq, bk, heads(b_v, B_HEADS), b_rpb).reshape(B, T, B_WIDTH)
    pos = jnp.arange(T, dtype=f32)
    cq = rope(rms_norm(heads(c_q, C_HEADS), c_qnorm_g), pos)
    ck = rope(rms_norm(heads(c_k, C_HEADS), c_knorm_g), pos)
    c_out = dilated_attention(cq, ck, heads(c_v, C_HEADS)).reshape(B, T, C_WIDTH)
    mix = jnp.concatenate([a_out, b_out, c_out], axis=-1).astype(h.dtype)
    h = h + mix @ w_out
    h = h + expert_choice_ffn(rms_norm(h, norm2_g), w_router, w_gate, w_up, w_down)
    e = rms_norm(p_i.astype(h.dtype) @ w_ple, ple_norm_g)
    gate = jax.nn.sigmoid((rms_norm(h, gate_norm_g) @ w_ple_gate).astype(f32))
    return h + (gate * e.astype(f32)).astype(h.dtype)


def _trunk(h, p, lb, weights):
    (norm1_g, w_in, a_norm_g, b_qnorm_g, b_knorm_g, b_rpb, c_qnorm_g, c_knorm_g, w_out,
     norm2_g, w_router, w_gate, w_up, w_down, w_ple, ple_norm_g, gate_norm_g, w_ple_gate) = weights
    for i in range(DEPTH):
        h = _layer(h, p[i], lb[0, i], lb[1, i], norm1_g[i], w_in[i], a_norm_g[i], b_qnorm_g[i],
                   b_knorm_g[i], b_rpb[i], c_qnorm_g[i], c_knorm_g[i], w_out[i], norm2_g[i],
                   w_router[i], w_gate[i], w_up[i], w_down[i], w_ple[i], ple_norm_g[i],
                   gate_norm_g[i], w_ple_gate[i])
    return h


def setup_inputs(seed: int = 0) -> dict:
    key = jax.random.key(seed)
    ks = jax.random.split(key, 24)
    nrm = lambda k, shape, s: jax.random.normal(k, shape, jnp.float32) * s
    gain = lambda k, shape: 1.0 + nrm(k, shape, 0.02)
    return {
        "x_prompt": nrm(ks[0], (BATCH, SEQ, D_MODEL), 1.0),
        "x_sample": nrm(ks[1], (DEC_BATCH, DEC_SEQ, D_MODEL), 1.0),
        "p_prompt": nrm(ks[2], (DEPTH, BATCH, SEQ, PLE_DIM), 1.0),
        "p_sample": nrm(ks[3], (DEPTH, DEC_BATCH, DEC_SEQ, PLE_DIM), 1.0),
        "norm1_g": gain(ks[4], (DEPTH, D_MODEL)),
        "w_in": nrm(ks[5], (DEPTH, D_MODEL, IN_WIDTH), D_MODEL ** -0.5),
        "lb_logits": nrm(ks[6], (2, DEPTH, A_WIDTH), 0.1),
        "a_norm_g": gain(ks[7], (DEPTH, HEAD_DIM)),
        "b_qnorm_g": gain(ks[8], (DEPTH, HEAD_DIM)),
        "b_knorm_g": gain(ks[9], (DEPTH, HEAD_DIM)),
        "b_rpb": nrm(ks[10], (DEPTH, B_HEADS, 2 * NA_WIN_R - 1, 2 * NA_WIN_C - 1), 0.1),
        "c_qnorm_g": gain(ks[11], (DEPTH, HEAD_DIM)),
        "c_knorm_g": gain(ks[12], (DEPTH, HEAD_DIM)),
        "w_out": nrm(ks[13], (DEPTH, MIX_WIDTH, D_MODEL), MIX_WIDTH ** -0.5),
        "norm2_g": gain(ks[14], (DEPTH, D_MODEL)),
        "w_router": nrm(ks[15], (DEPTH, D_MODEL, N_EXPERTS), D_MODEL ** -0.5),
        "w_gate": nrm(ks[16], (DEPTH, N_EXPERTS, D_MODEL, EXPERT_FF), D_MODEL ** -0.5),
        "w_up": nrm(ks[17], (DEPTH, N_EXPERTS, D_MODEL, EXPERT_FF), D_MODEL ** -0.5),
        "w_down": nrm(ks[18], (DEPTH, N_EXPERTS, EXPERT_FF, D_MODEL), EXPERT_FF ** -0.5),
        "w_ple": nrm(ks[19], (DEPTH, PLE_DIM, D_MODEL), PLE_DIM ** -0.5),
        "ple_norm_g": gain(ks[20], (DEPTH, D_MODEL)),
        "gate_norm_g": gain(ks[21], (DEPTH, D_MODEL)),
        "w_ple_gate": nrm(ks[22], (DEPTH, D_MODEL, D_MODEL), D_MODEL ** -0.5),
    }


def reference(x_prompt, x_sample, p_prompt, p_sample, norm1_g, w_in, lb_logits, a_norm_g,
              b_qnorm_g, b_knorm_g, b_rpb, c_qnorm_g, c_knorm_g, w_out, norm2_g, w_router,
              w_gate, w_up, w_down, w_ple, ple_norm_g, gate_norm_g, w_ple_gate):
    pr = jax.nn.softmax(lb_logits.astype(jnp.float32), axis=1)
    lb = jnp.cumsum(pr, axis=1) - pr[:, :1]
    weights = (norm1_g, w_in, a_norm_g, b_qnorm_g, b_knorm_g, b_rpb, c_qnorm_g, c_knorm_g, w_out,
               norm2_g, w_router, w_gate, w_up, w_down, w_ple, ple_norm_g, gate_norm_g, w_ple_gate)
    y_prompt = _trunk(x_prompt, p_prompt, lb, weights)
    y_sample = _trunk(x_sample, p_sample, lb, weights)
    return (y_prompt, y_sample)
```

```python
import functools

import jax
import jax.numpy as jnp
import numpy as np
from jax import lax
from jax.experimental import pallas as pl
from jax.experimental.pallas import tpu as pltpu

D_MODEL = 2048
DEPTH = 4
HEAD_DIM = 128
A_HEADS = 6
B_HEADS = 4
C_HEADS = 6
A_WIDTH = A_HEADS * HEAD_DIM
B_WIDTH = B_HEADS * HEAD_DIM
C_WIDTH = C_HEADS * HEAD_DIM
IN_WIDTH = 5 * A_WIDTH + 3 * B_WIDTH + 3 * C_WIDTH
SPLIT_POINTS = [int(s) for s in np.cumsum([A_WIDTH] * 5 + [B_WIDTH] * 3 + [C_WIDTH] * 2)]
HGRN_CHUNK = 64
GRID_W = 64
NA_WIN_R = 8
NA_WIN_C = 16
NA_QBLK = 16
NA_BAND = 32
DIL_GROUPS = ((128, 1), (512, 4), (2048, 16))
C_HEADS_PER_GROUP = C_HEADS // len(DIL_GROUPS)
DIL_BLK = 64
ROPE_THETA = 10000.0
N_EXPERTS = 16
EC_CAPACITY_FACTOR = 2
EPS = 1e-6

VMEM_LIMIT_BYTES = 56 * 1024 * 1024


def _norm_matmul_kernel(x_ref, g_ref, w_ref, o_ref, xn_ref, *, normalize):
    @pl.when(pl.program_id(1) == 0)
    def _():
        x = x_ref[...]
        if normalize:
            ms = jnp.mean(x * x, axis=-1, keepdims=True)
            x = x * lax.rsqrt(ms + EPS) * g_ref[...]
        xn_ref[...] = x.astype(jnp.bfloat16)

    o_ref[...] = jnp.dot(xn_ref[...], w_ref[...], preferred_element_type=jnp.float32)


def _matmul(x, w, gain=None, *, tm=1024, tn=512):
    n, k = x.shape
    m = w.shape[1]
    tm = min(tm, n)
    tn = min(tn, m)
    assert n % tm == 0 and m % tn == 0
    normalize = gain is not None
    g = (gain if normalize else jnp.ones((k,), jnp.float32)).reshape(1, k).astype(jnp.float32)
    return pl.pallas_call(
        functools.partial(_norm_matmul_kernel, normalize=normalize),
        out_shape=jax.ShapeDtypeStruct((n, m), jnp.float32),
        grid=(n // tm, m // tn),
        in_specs=[
            pl.BlockSpec((tm, k), lambda i, j: (i, 0)),
            pl.BlockSpec((1, k), lambda i, j: (0, 0)),
            pl.BlockSpec((k, tn), lambda i, j: (0, j)),
        ],
        out_specs=pl.BlockSpec((tm, tn), lambda i, j: (i, j)),
        scratch_shapes=[pltpu.VMEM((tm, k), jnp.bfloat16)],
        compiler_params=pltpu.CompilerParams(
            dimension_semantics=("arbitrary", "arbitrary"),
            vmem_limit_bytes=VMEM_LIMIT_BYTES,
        ),
        name="norm_matmul" if normalize else "matmul",
    )(x, g, w.astype(jnp.bfloat16))


def _rms_norm(x, g):
    x32 = x.astype(jnp.float32)
    y = x32 * lax.rsqrt(jnp.mean(x32 * x32, axis=-1, keepdims=True) + EPS) * g.astype(jnp.float32)
    return y.astype(x.dtype)


def _rope(x, pos):
    half = x.shape[-1] // 2
    inv = 1.0 / (ROPE_THETA ** (jnp.arange(half, dtype=jnp.float32) * 2.0 / x.shape[-1]))
    ang = pos[:, None] * inv[None, :]
    cos, sin = jnp.cos(ang)[:, None, :], jnp.sin(ang)[:, None, :]
    x1, x2 = x[..., :half], x[..., half:]
    return jnp.concatenate([x1 * cos - x2 * sin, x2 * cos + x1 * sin], axis=-1)


def _hgrn2_scan(q, k, v, log_f):
    B, H, T, dk = q.shape
    dv = v.shape[-1]
    n = T // HGRN_CHUNK

    def chunks(a):
        return jnp.moveaxis(a.reshape(B, H, n, HGRN_CHUNK, a.shape[-1]), 2, 0)

    tri = jnp.tril(jnp.ones((HGRN_CHUNK, HGRN_CHUNK), dtype=bool))

    def step(S, inp):
        qc, kc, vc, lf = inp
        b = jnp.cumsum(lf, axis=-2)
        diff = b[:, :, :, None, :] - b[:, :, None, :, :]
        decay = jnp.exp(jnp.where(tri[:, :, None], diff, -jnp.inf))
        scores = jnp.einsum('bhtd,bhsd,bhtsd->bhts', qc, kc, decay)
        o = jnp.einsum('bhts,bhsv->bhtv', scores, vc) + jnp.einsum('bhtd,bhdv->bhtv', qc * jnp.exp(b), S)
        b_last = b[:, :, -1, :]
        S = jnp.exp(b_last)[..., None] * S + jnp.einsum(
            'bhsd,bhsv->bhdv', kc * jnp.exp(b_last[:, :, None, :] - b), vc)
        return S, o

    S0 = jnp.zeros((B, H, dk, dv), jnp.float32)
    _, o = lax.scan(step, S0, (chunks(q), chunks(k), chunks(v), chunks(log_f)))
    return jnp.moveaxis(o, 0, 2).reshape(B, H, T, dv)


def _hgrn2_mixer(a_q, a_ff, a_fb, a_i, a_g, lb_f, lb_b, norm_g):
    B, T, _ = a_q.shape

    def heads_t(a):
        return a.astype(jnp.float32).reshape(B, T, A_HEADS, HEAD_DIM).transpose(0, 2, 1, 3)

    q = jax.nn.silu(heads_t(a_q))
    v = heads_t(a_i)

    def gate(f_pre, lb):
        lb = lb.astype(jnp.float32).reshape(A_HEADS, 1, HEAD_DIM)
        f = lb + (1.0 - lb) * jax.nn.sigmoid(heads_t(f_pre))
        return jnp.log(f), 1.0 - f

    lf_f, k_f = gate(a_ff, lb_f)
    lf_b, k_b = gate(a_fb, lb_b)
    fl = lambda a: jnp.flip(a, axis=2)
    o = _hgrn2_scan(q, k_f, v, lf_f) + fl(_hgrn2_scan(fl(q), fl(k_b), fl(v), fl(lf_b)))
    o = o.transpose(0, 2, 1, 3)
    o = _rms_norm(o, norm_g) * jax.nn.silu(a_g.astype(jnp.float32).reshape(B, T, A_HEADS, HEAD_DIM))
    return o.reshape(B, T, A_WIDTH)


def _neighborhood_attention(q, k, v, rpb):
    B, T, H, dh = q.shape
    rows = T // GRID_W
    wr = min(NA_WIN_R, rows)
    to_grid = lambda a: a.reshape(B, rows, GRID_W, H, dh).transpose(0, 3, 1, 2, 4)
    qg, kg, vg = to_grid(q), to_grid(k), to_grid(v)
    n_cb = GRID_W // NA_QBLK
    cb = np.arange(n_cb)
    band_start = np.clip(cb * NA_QBLK - NA_WIN_C // 2, 0, GRID_W - NA_BAND)
    band_cols = band_start[:, None] + np.arange(NA_BAND)
    q_cols = cb[:, None] * NA_QBLK + np.arange(NA_QBLK)
    win_start = np.clip(q_cols - NA_WIN_C // 2, 0, GRID_W - NA_WIN_C)
    kc = band_cols[:, None, :]
    col_ok = (kc >= win_start[..., None]) & (kc < win_start[..., None] + NA_WIN_C)
    dc_idx = np.clip(kc - q_cols[..., None] + NA_WIN_C - 1, 0, 2 * NA_WIN_C - 2)
    rpb_c = rpb.astype(jnp.float32)[:, :, dc_idx]
    col_ok = jnp.asarray(col_ok)[:, :, None, :]
    scale = 1.0 / np.sqrt(dh)

    def one_row(r):
        rs = jnp.clip(r - wr // 2, 0, rows - wr)
        k_band = lax.dynamic_slice_in_dim(kg, rs, wr, axis=2)[:, :, :, band_cols]
        v_band = lax.dynamic_slice_in_dim(vg, rs, wr, axis=2)[:, :, :, band_cols]
        q_row = lax.dynamic_index_in_dim(qg, r, axis=2, keepdims=False).reshape(B, H, n_cb, NA_QBLK, dh)
        s = jnp.einsum('bhjqd,bhrjkd->bhjqrk', q_row, k_band) * scale
        dr_idx = rs - r + jnp.arange(wr) + NA_WIN_R - 1
        bias = jnp.take(rpb_c, dr_idx, axis=1).transpose(0, 2, 3, 1, 4)
        s = jnp.where(col_ok, s + bias, -jnp.inf)
        p = jax.nn.softmax(s.reshape(B, H, n_cb, NA_QBLK, wr * NA_BAND), axis=-1).reshape(s.shape)
        o = jnp.einsum('bhjqrk,bhrjkd->bhjqd', p, v_band)
        return o.reshape(B, H, GRID_W, dh)

    out = lax.map(one_row, jnp.arange(rows))
    return out.transpose(1, 0, 3, 2, 4).reshape(B, T, H, dh)


def _banded_attention(q, k, v, half):
    L, dh = q.shape[-2], q.shape[-1]
    lead = q.shape[:-2]
    nb = -(-L // DIL_BLK)
    Lp = nb * DIL_BLK
    zp = [(0, 0)] * len(lead)
    qb = jnp.pad(q, zp + [(0, Lp - L), (0, 0)]).reshape(*lead, nb, DIL_BLK, dh)

    def windows(a):
        ap = jnp.pad(a, zp + [(DIL_BLK, Lp - L + DIL_BLK), (0, 0)]).reshape(*lead, nb + 2, DIL_BLK, dh)
        return jnp.concatenate([ap[..., :-2, :, :], ap[..., 1:-1, :, :], ap[..., 2:, :, :]], axis=-2)

    kw, vw = windows(k), windows(v)
    qpos = np.arange(nb)[:, None] * DIL_BLK + np.arange(DIL_BLK)
    kpos = (np.arange(nb)[:, None] - 1) * DIL_BLK + np.arange(3 * DIL_BLK)
    mask = (np.abs(qpos[:, :, None] - kpos[:, None, :]) <= half) & ((kpos >= 0) & (kpos < L))[:, None, :]
    s = jnp.einsum('...nqd,...nkd->...nqk', qb, kw) * (1.0 / np.sqrt(dh))
    s = jnp.where(jnp.asarray(mask), s, -jnp.inf)
    m = jnp.max(s, axis=-1, keepdims=True)
    p = jnp.exp(s - m)
    den = jnp.sum(p, axis=-1, keepdims=True)
    o = jnp.einsum('...nqk,...nkd->...nqd', p, vw) / den
    lse = (m + jnp.log(den))[..., 0]
    return o.reshape(*lead, Lp, dh)[..., :L, :], lse.reshape(*lead, Lp)[..., :L]


def _dilated_attention(q, k, v):
    B, T, _, dh = q.shape
    hp = C_HEADS_PER_GROUP
    outs, lses = [], []
    for g, (w, d) in enumerate(DIL_GROUPS):
        sl = slice(g * hp, (g + 1) * hp)
        dil = lambda a: a[:, :, sl].reshape(B, T // d, d, hp, dh).transpose(0, 2, 3, 1, 4)
        o, lse = _banded_attention(dil(q), dil(k), dil(v), w // (2 * d))
        outs.append(o.transpose(0, 3, 1, 2, 4).reshape(B, T, hp, dh))
        lses.append(lse.transpose(0, 3, 1, 2).reshape(B, T, hp))
    alpha = jax.nn.softmax(jnp.stack(lses, axis=0), axis=0)
    return jnp.concatenate([o * alpha[g][..., None] for g, o in enumerate(outs)], axis=2)


def _expert_choice_ffn(xf, w_router, w_gate, w_up, w_down):
    N, D = xf.shape
    aff = jax.nn.softmax((xf @ w_router).astype(jnp.float32), axis=-1)
    cap = (EC_CAPACITY_FACTOR * N) // N_EXPERTS
    gates, idx = lax.top_k(aff.T, cap)
    xe = xf[idx]
    hdn = jax.nn.silu(jnp.einsum('ecd,edf->ecf', xe, w_gate)) * jnp.einsum('ecd,edf->ecf', xe, w_up)
    ye = jnp.einsum('ecf,efd->ecd', hdn, w_down) * gates[..., None]
    return jnp.zeros((N, D), xf.dtype).at[idx.reshape(-1)].add(ye.reshape(-1, D))


def _layer(h, p_i, lb_f, lb_b, norm1_g, w_in, a_norm_g, b_qnorm_g, b_knorm_g, b_rpb,
           c_qnorm_g, c_knorm_g, w_out, norm2_g, w_router, w_gate, w_up, w_down,
           w_ple, ple_norm_g, gate_norm_g, w_ple_gate):
    B, T, _ = h.shape
    N = B * T
    f32 = jnp.float32
    hf = h.reshape(N, D_MODEL)
    z = _matmul(hf, w_in, norm1_g).reshape(B, T, IN_WIDTH)
    (a_q, a_ff, a_fb, a_i, a_g, b_q, b_k, b_v, c_q, c_k, c_v) = jnp.split(z, SPLIT_POINTS, axis=-1)
    a_out = _hgrn2_mixer(a_q, a_ff, a_fb, a_i, a_g, lb_f, lb_b, a_norm_g)
    heads = lambda a, nh: a.astype(f32).reshape(B, T, nh, HEAD_DIM)
    bq = _rms_norm(heads(b_q, B_HEADS), b_qnorm_g)
    bk = _rms_norm(heads(b_k, B_HEADS), b_knorm_g)
    b_out = _neighborhood_attention(bq, bk, heads(b_v, B_HEADS), b_rpb).reshape(B, T, B_WIDTH)
    pos = jnp.arange(T, dtype=f32)
    cq = _rope(_rms_norm(heads(c_q, C_HEADS), c_qnorm_g), pos)
    ck = _rope(_rms_norm(heads(c_k, C_HEADS), c_knorm_g), pos)
    c_out = _dilated_attention(cq, ck, heads(c_v, C_HEADS)).reshape(B, T, C_WIDTH)
    mix = jnp.concatenate([a_out, b_out, c_out], axis=-1).reshape(N, D_MODEL)
    hf = hf + _matmul(mix, w_out)
    hf = hf + _expert_choice_ffn(_rms_norm(hf, norm2_g), w_router, w_gate, w_up, w_down)
    e = _rms_norm(_matmul(p_i.reshape(N, -1), w_ple), ple_norm_g)
    gate = jax.nn.sigmoid(_matmul(hf, w_ple_gate, gate_norm_g))
    return (hf + gate * e).reshape(B, T, D_MODEL)


def _trunk(h, p, lb, weights):
    (norm1_g, w_in, a_norm_g, b_qnorm_g, b_knorm_g, b_rpb, c_qnorm_g, c_knorm_g, w_out,
     norm2_g, w_router, w_gate, w_up, w_down, w_ple, ple_norm_g, gate_norm_g, w_ple_gate) = weights
    for i in range(DEPTH):
        h = _layer(h, p[i], lb[0, i], lb[1, i], norm1_g[i], w_in[i], a_norm_g[i], b_qnorm_g[i],
                   b_knorm_g[i], b_rpb[i], c_qnorm_g[i], c_knorm_g[i], w_out[i], norm2_g[i],
                   w_router[i], w_gate[i], w_up[i], w_down[i], w_ple[i], ple_norm_g[i],
                   gate_norm_g[i], w_ple_gate[i])
    return h


def kernel(x_prompt, x_sample, p_prompt, p_sample, norm1_g, w_in, lb_logits, a_norm_g,
           b_qnorm_g, b_knorm_g, b_rpb, c_qnorm_g, c_knorm_g, w_out, norm2_g, w_router,
           w_gate, w_up, w_down, w_ple, ple_norm_g, gate_norm_g, w_ple_gate):
    pr = jax.nn.softmax(lb_logits.astype(jnp.float32), axis=1)
    lb = jnp.cumsum(pr, axis=1) - pr[:, :1]
    weights = (norm1_g, w_in, a_norm_g, b_qnorm_g, b_knorm_g, b_rpb, c_qnorm_g, c_knorm_g, w_out,
               norm2_g, w_router, w_gate, w_up, w_down, w_ple, ple_norm_g, gate_norm_g, w_ple_gate)
    y_prompt = _trunk(x_prompt, p_prompt, lb, weights)
    y_sample = _trunk(x_sample, p_sample, lb, weights)
    return (y_prompt, y_sample)
```

```python
import functools

import jax
import jax.numpy as jnp
import numpy as np
from jax import lax
from jax.experimental import pallas as pl
from jax.experimental.pallas import tpu as pltpu

D_MODEL = 2048
DEPTH = 4
HEAD_DIM = 128
A_HEADS = 6
B_HEADS = 4
C_HEADS = 6
A_WIDTH = A_HEADS * HEAD_DIM
B_WIDTH = B_HEADS * HEAD_DIM
C_WIDTH = C_HEADS * HEAD_DIM
IN_WIDTH = 5 * A_WIDTH + 3 * B_WIDTH + 3 * C_WIDTH
HGRN_CHUNK = 64
GRID_W = 64
NA_WIN_R = 8
NA_WIN_C = 16
DIL_GROUPS = ((128, 1), (512, 4), (2048, 16))
C_HEADS_PER_GROUP = C_HEADS // len(DIL_GROUPS)
ROPE_THETA = 10000.0
N_EXPERTS = 16
EC_CAPACITY_FACTOR = 2
EPS = 1e-6

VMEM_LIMIT_BYTES = 56 * 1024 * 1024
NEG_INF = -1e30
Z_COLS = IN_WIDTH // HEAD_DIM
B_COL0 = 5 * A_HEADS
C_COL0 = B_COL0 + 3 * B_HEADS


def _sigmoid(x):
    return 1.0 / (1.0 + jnp.exp(-x))


def _bf16(x):
    return x.astype(jnp.bfloat16)


def _dot(a, b):
    return jnp.dot(_bf16(a), _bf16(b), preferred_element_type=jnp.float32)


def _dot_nt(a, b):
    return lax.dot_general(_bf16(a), _bf16(b), (((1,), (1,)), ((), ())),
                           preferred_element_type=jnp.float32)


def _dot_tn(a, b):
    return lax.dot_general(_bf16(a), _bf16(b), (((0,), (0,)), ((), ())),
                           preferred_element_type=jnp.float32)


def _rms(x, g):
    return x * lax.rsqrt(jnp.mean(x * x, axis=-1, keepdims=True) + EPS) * g


def _norm_matmul_kernel(x_ref, g_ref, w_ref, o_ref, xn_ref, *, normalize):
    @pl.when(pl.program_id(1) == 0)
    def _():
        x = x_ref[...]
        if normalize:
            x = _rms(x, g_ref[...])
        xn_ref[...] = x.astype(jnp.bfloat16)

    o_ref[...] = jnp.dot(xn_ref[...], w_ref[...], preferred_element_type=jnp.float32)


def _matmul(x, w, gain=None, *, tm=1024, tn=512):
    n, k = x.shape
    m = w.shape[1]
    tm = min(tm, n)
    tn = min(tn, m)
    assert n % tm == 0 and m % tn == 0
    normalize = gain is not None
    g = (gain if normalize else jnp.ones((k,), jnp.float32)).reshape(1, k).astype(jnp.float32)
    return pl.pallas_call(
        functools.partial(_norm_matmul_kernel, normalize=normalize),
        out_shape=jax.ShapeDtypeStruct((n, m), jnp.float32),
        grid=(n // tm, m // tn),
        in_specs=[
            pl.BlockSpec((tm, k), lambda i, j: (i, 0)),
            pl.BlockSpec((1, k), lambda i, j: (0, 0)),
            pl.BlockSpec((k, tn), lambda i, j: (0, j)),
        ],
        out_specs=pl.BlockSpec((tm, tn), lambda i, j: (i, j)),
        scratch_shapes=[pltpu.VMEM((tm, k), jnp.bfloat16)],
        compiler_params=pltpu.CompilerParams(
            dimension_semantics=("arbitrary", "arbitrary"),
            vmem_limit_bytes=VMEM_LIMIT_BYTES,
        ),
        name="norm_matmul" if normalize else "matmul",
    )(x, g, w.astype(jnp.bfloat16))


HGRN_SUB = 16
HGRN_NSUB = HGRN_CHUNK // HGRN_SUB
HGRN_HPS = 2
HGRN_COLS = HGRN_HPS * HEAD_DIM


def _hgrn_chunk(q, x, v, lb, st, reverse):
    C, S = HGRN_CHUNK, HGRN_SUB
    f = lb + (1.0 - lb) * _sigmoid(x)
    lf = jnp.log(f)
    k = 1.0 - f
    row = lax.broadcasted_iota(jnp.int32, (C, C), 0)
    col = lax.broadcasted_iota(jnp.int32, (C, C), 1)
    tri = jnp.where((col >= row) if reverse else (col <= row), 1.0, 0.0).astype(jnp.bfloat16)
    hi = _bf16(lf)
    r1 = lf - hi.astype(jnp.float32)
    mid = _bf16(r1)
    lo = _bf16(r1 - mid.astype(jnp.float32))
    b = (jnp.dot(tri, hi, preferred_element_type=jnp.float32)
         + jnp.dot(tri, mid, preferred_element_type=jnp.float32)
         + jnp.dot(tri, lo, preferred_element_type=jnp.float32))

    s_iota = lax.broadcasted_iota(jnp.int32, (S, HEAD_DIM), 0)
    ones = jnp.ones((HEAD_DIM, HEAD_DIM), jnp.bfloat16)
    sel_r = lax.broadcasted_iota(jnp.int32, (S, S * S), 0)
    sel_c = lax.broadcasted_iota(jnp.int32, (S, S * S), 1)
    sel = jnp.where((sel_c >= sel_r * S) & (sel_c < sel_r * S + S), 1.0, 0.0).astype(jnp.bfloat16)

    outs = []
    for i in range(HGRN_NSUB):
        lo_r, hi_r = i * S, (i + 1) * S
        b_i, q_i, k_i, v_i = b[lo_r:hi_r], q[lo_r:hi_r], k[lo_r:hi_r], v[lo_r:hi_r]
        w_rows = []
        for t in range(S):
            d = b_i[t:t + 1, :] - b_i
            ok = (s_iota >= t) if reverse else (s_iota <= t)
            w_rows.append(jnp.exp(jnp.where(ok, d, NEG_INF)) * k_i * q_i[t:t + 1, :])
        w = jnp.concatenate(w_rows, axis=0)
        a_ts = jnp.dot(_bf16(w), ones, preferred_element_type=jnp.float32)
        o_i = jnp.dot(sel, _bf16(a_ts * jnp.concatenate([v_i] * S, axis=0)),
                      preferred_element_type=jnp.float32)
        if reverse and i < HGRN_NSUB - 1:
            e_lo, e_hi, beta = hi_r, C, b[hi_r:hi_r + 1, :]
        elif (not reverse) and i > 0:
            e_lo, e_hi, beta = 0, lo_r, b[lo_r - 1:lo_r, :]
        else:
            e_lo = None
        if e_lo is not None:
            qh = q_i * jnp.exp(b_i - beta)
            kh = k[e_lo:e_hi] * jnp.exp(beta - b[e_lo:e_hi])
            o_i = o_i + _dot(_dot_nt(qh, kh), v[e_lo:e_hi])
        outs.append(o_i)
    o = jnp.concatenate(outs, axis=0) + _dot_nt(q * jnp.exp(b), st)
    b_last = b[0:1, :] if reverse else b[C - 1:C, :]
    st_new = st * jnp.exp(b_last) + _dot_tn(v, k * jnp.exp(b_last - b))
    return o, st_new


def _hgrn_kernel(q_ref, ff_ref, fb_ref, v_ref, g_ref, lb_ref, ng_ref, o_ref, ofwd_ref, st_ref,
                 *, n_tiles, tile_rows):
    s = pl.program_id(2)
    n_chunks = tile_rows // HGRN_CHUNK

    @pl.when((s == 0) | (s == n_tiles))
    def _():
        st_ref[...] = jnp.zeros_like(st_ref)

    def chunk_all_heads(r0, f_ref, direction, reverse):
        rows = pl.ds(r0, HGRN_CHUNK)
        outs = []
        for j in range(HGRN_HPS):
            cols = slice(j * HEAD_DIM, (j + 1) * HEAD_DIM)
            qp = q_ref[rows, cols]
            o, st = _hgrn_chunk(qp * _sigmoid(qp), f_ref[rows, cols], v_ref[rows, cols],
                                lb_ref[direction:direction + 1, cols], st_ref[j], reverse)
            st_ref[j] = st
            outs.append(o)
        return outs

    @pl.when(s < n_tiles)
    def _():
        base = s * tile_rows

        def body(c, carry):
            r0 = pl.multiple_of(c * HGRN_CHUNK, HGRN_CHUNK)
            outs = chunk_all_heads(r0, ff_ref, 0, False)
            dst = pl.ds(pl.multiple_of(base + r0, HGRN_CHUNK), HGRN_CHUNK)
            for j in range(HGRN_HPS):
                ofwd_ref[dst, j * HEAD_DIM:(j + 1) * HEAD_DIM] = outs[j]
            return carry

        lax.fori_loop(0, n_chunks, body, 0)

    @pl.when(s >= n_tiles)
    def _():
        base = (2 * n_tiles - 1 - s) * tile_rows

        def body(ci, carry):
            r0 = pl.multiple_of((n_chunks - 1 - ci) * HGRN_CHUNK, HGRN_CHUNK)
            outs = chunk_all_heads(r0, fb_ref, 1, True)
            src = pl.ds(pl.multiple_of(base + r0, HGRN_CHUNK), HGRN_CHUNK)
            for j in range(HGRN_HPS):
                cols = slice(j * HEAD_DIM, (j + 1) * HEAD_DIM)
                y = _rms(outs[j] + ofwd_ref[src, cols], ng_ref[...])
                g = g_ref[pl.ds(r0, HGRN_CHUNK), cols]
                o_ref[pl.ds(r0, HGRN_CHUNK), cols] = y * (g * _sigmoid(g))
            return carry

        lax.fori_loop(0, n_chunks, body, 0)


def _hgrn_tile_rows(T):
    return min(T, 2048)


def _hgrn2_mixer(z, lb_f, lb_b, norm_g):
    B, T, _ = z.shape
    tile_rows = _hgrn_tile_rows(T)
    n_tiles = T // tile_rows
    n_groups = A_HEADS // HGRN_HPS
    lb = jnp.stack([lb_f.reshape(n_groups, HGRN_COLS), lb_b.reshape(n_groups, HGRN_COLS)], axis=1)
    last = n_tiles - 1

    def both(s):
        return jnp.where(s < n_tiles, s, 2 * n_tiles - 1 - s)

    def fwd_only(s):
        return jnp.minimum(s, last)

    def bwd_only(s):
        return jnp.where(s < n_tiles, last, 2 * n_tiles - 1 - s)

    def zspec(part, tile_of):
        return pl.BlockSpec((None, tile_rows, HGRN_COLS),
                            lambda b, h, s: (b, tile_of(s), part * n_groups + h))

    return pl.pallas_call(
        functools.partial(_hgrn_kernel, n_tiles=n_tiles, tile_rows=tile_rows),
        out_shape=jax.ShapeDtypeStruct((B, T, A_WIDTH), jnp.float32),
        grid=(B, n_groups, 2 * n_tiles),
        in_specs=[
            zspec(0, both), zspec(1, fwd_only), zspec(2, bwd_only), zspec(3, both), zspec(4, bwd_only),
            pl.BlockSpec((None, 2, HGRN_COLS), lambda b, h, s: (h, 0, 0)),
            pl.BlockSpec((1, HEAD_DIM), lambda b, h, s: (0, 0)),
        ],
        out_specs=pl.BlockSpec((None, tile_rows, HGRN_COLS), lambda b, h, s: (b, bwd_only(s), h)),
        scratch_shapes=[pltpu.VMEM((T, HGRN_COLS), jnp.float32),
                        pltpu.VMEM((HGRN_HPS, HEAD_DIM, HEAD_DIM), jnp.float32)],
        compiler_params=pltpu.CompilerParams(
            dimension_semantics=("arbitrary", "arbitrary", "arbitrary"),
            vmem_limit_bytes=VMEM_LIMIT_BYTES,
        ),
        name="hgrn2",
    )(z, z, z, z, z, lb.astype(jnp.float32), norm_g.reshape(1, HEAD_DIM).astype(jnp.float32))


NA_KEYS = NA_WIN_R * GRID_W


def _na_bias_table(rpb):
    qc = np.arange(GRID_W)[:, None]
    kc = np.arange(GRID_W)[None, :]
    ws = np.clip(qc - NA_WIN_C // 2, 0, GRID_W - NA_WIN_C)
    col_ok = (kc >= ws) & (kc < ws + NA_WIN_C)
    dc = np.clip(kc - qc + NA_WIN_C - 1, 0, 2 * NA_WIN_C - 2)
    toe = jnp.where(jnp.asarray(col_ok)[None, None], rpb.astype(jnp.float32)[:, :, dc], NEG_INF)
    cases = []
    for c in range(NA_WIN_R):
        cases.append(jnp.concatenate([toe[:, c + j] for j in range(NA_WIN_R)], axis=-1))
    return jnp.stack(cases, axis=1)


def _na_kernel(q_ref, k_ref, v_ref, bias_ref, qg_ref, kg_ref, o_ref, *, rows, tile_grid_rows):
    qt = pl.program_id(2)

    def body(rl, carry):
        r = qt * tile_grid_rows + rl
        rs = jnp.clip(r - NA_WIN_R // 2, 0, rows - NA_WIN_R)
        q0 = pl.multiple_of(rl * GRID_W, GRID_W)
        k0 = pl.multiple_of(rs * GRID_W, GRID_W)
        q = _rms(q_ref[pl.ds(q0, GRID_W), :], qg_ref[...]) * (1.0 / np.sqrt(HEAD_DIM))
        k = _rms(k_ref[pl.ds(k0, NA_KEYS), :], kg_ref[...])
        s = _dot_nt(q, k) + bias_ref[rs - r + NA_WIN_R - 1]
        m = jnp.max(s, axis=-1, keepdims=True)
        p = jnp.exp(s - m)
        den = jnp.sum(p, axis=-1, keepdims=True)
        o_ref[pl.ds(q0, GRID_W), :] = _dot(p, v_ref[pl.ds(k0, NA_KEYS), :]) / den
        return carry

    lax.fori_loop(0, tile_grid_rows, body, 0, unroll=2)


def _neighborhood_mixer(z, rpb, qnorm_g, knorm_g):
    B, T, _ = z.shape
    rows = T // GRID_W
    assert rows >= NA_WIN_R
    tile_grid_rows = min(rows, 32)
    tq = tile_grid_rows * GRID_W
    bias = _na_bias_table(rpb)
    gq = qnorm_g.reshape(1, HEAD_DIM).astype(jnp.float32)
    gk = knorm_g.reshape(1, HEAD_DIM).astype(jnp.float32)
    return pl.pallas_call(
        functools.partial(_na_kernel, rows=rows, tile_grid_rows=tile_grid_rows),
        out_shape=jax.ShapeDtypeStruct((B, T, B_WIDTH), jnp.float32),
        grid=(B, B_HEADS, T // tq),
        in_specs=[
            pl.BlockSpec((None, tq, HEAD_DIM), lambda b, h, t: (b, t, B_COL0 + h)),
            pl.BlockSpec((None, T, HEAD_DIM), lambda b, h, t: (b, 0, B_COL0 + B_HEADS + h)),
            pl.BlockSpec((None, T, HEAD_DIM), lambda b, h, t: (b, 0, B_COL0 + 2 * B_HEADS + h)),
            pl.BlockSpec((None, NA_WIN_R, GRID_W, NA_KEYS), lambda b, h, t: (h, 0, 0, 0)),
            pl.BlockSpec((1, HEAD_DIM), lambda b, h, t: (0, 0)),
            pl.BlockSpec((1, HEAD_DIM), lambda b, h, t: (0, 0)),
        ],
        out_specs=pl.BlockSpec((None, tq, HEAD_DIM), lambda b, h, t: (b, t, h)),
        compiler_params=pltpu.CompilerParams(
            dimension_semantics=("arbitrary", "arbitrary", "arbitrary"),
            vmem_limit_bytes=VMEM_LIMIT_BYTES,
        ),
        name="neighborhood_attn",
    )(z, z, z, bias, gq, gk)


DIL_HALF = 64
DIL_QB = 128
assert all(w // (2 * d) == DIL_HALF for w, d in DIL_GROUPS)


def _rope_tables(T):
    half = HEAD_DIM // 2
    inv = 1.0 / (ROPE_THETA ** (jnp.arange(half, dtype=jnp.float32) * 2.0 / HEAD_DIM))
    ang = jnp.arange(T, dtype=jnp.float32)[:, None] * inv[None, :]
    cos, sin = jnp.cos(ang), jnp.sin(ang)
    return jnp.concatenate([cos, cos], axis=-1), jnp.concatenate([-sin, sin], axis=-1)


def _dil_kernel(q_ref, k_ref, v_ref, qcos_ref, qsin_ref, kcos_ref, ksin_ref, qg_ref, kg_ref,
                o_ref, lse_ref, kn_ref, *, L, Lq, KW, QB, KB):
    qi = pl.program_id(3)

    def norm_rope(x, g, cos, sin):
        xn = _rms(x, g)
        return xn * cos + pltpu.roll(xn, HEAD_DIM // 2, 1) * sin

    @pl.when(qi == 0)
    def _():
        def kbody(i, carry):
            r = pl.ds(pl.multiple_of(i * KB, KB), KB)
            kn_ref[r, :] = _bf16(norm_rope(k_ref[r, :], kg_ref[...], kcos_ref[r, :], ksin_ref[r, :]))
            return carry

        lax.fori_loop(0, L // KB, kbody, 0)

    def body(i, carry):
        q0l = pl.multiple_of(i * QB, QB)
        q0 = qi * Lq + q0l
        kb = pl.multiple_of(jnp.clip(q0 - DIL_HALF, 0, L - KW), DIL_HALF)
        rows = pl.ds(q0l, QB)
        q = norm_rope(q_ref[rows, :], qg_ref[...], qcos_ref[rows, :], qsin_ref[rows, :])
        s = _dot_nt(q * (1.0 / np.sqrt(HEAD_DIM)), kn_ref[pl.ds(kb, KW), :])
        qpos = q0 + lax.broadcasted_iota(jnp.int32, (QB, KW), 0)
        kpos = kb + lax.broadcasted_iota(jnp.int32, (QB, KW), 1)
        s = jnp.where(jnp.abs(qpos - kpos) <= DIL_HALF, s, NEG_INF)
        m = jnp.max(s, axis=-1, keepdims=True)
        p = jnp.exp(s - m)
        den = jnp.sum(p, axis=-1, keepdims=True)
        o_ref[rows, :] = _dot(p, v_ref[pl.ds(kb, KW), :]) / den
        lse_ref[rows, :] = jnp.broadcast_to(m + jnp.log(den), (QB, HEAD_DIM))
        return carry

    lax.fori_loop(0, Lq // QB, body, 0, unroll=2 if Lq // QB >= 2 else 1)


def _dilated_group(z, g, d, qnorm_g, knorm_g, cos2, sin2):
    B, T, _ = z.shape
    hp = C_HEADS_PER_GROUP
    L = T // d
    Lq = min(L, 1024)
    QB = min(DIL_QB, Lq)
    KW = min(2 * DIL_QB, L)
    KB = min(L, 512)
    zv = z.reshape(B, L, d * IN_WIDTH)
    cosv = cos2.reshape(L, d * HEAD_DIM)
    sinv = sin2.reshape(L, d * HEAD_DIM)
    gq = qnorm_g.reshape(1, HEAD_DIM).astype(jnp.float32)
    gk = knorm_g.reshape(1, HEAD_DIM).astype(jnp.float32)

    def zcol(part):
        return lambda b, j, h, qi: j * Z_COLS + C_COL0 + part * C_HEADS + g * hp + h

    out_sds = jax.ShapeDtypeStruct((B, L, d * hp * HEAD_DIM), jnp.float32)
    out_spec = pl.BlockSpec((None, Lq, HEAD_DIM), lambda b, j, h, qi: (b, qi, j * hp + h))
    o, lse = pl.pallas_call(
        functools.partial(_dil_kernel, L=L, Lq=Lq, KW=KW, QB=QB, KB=KB),
        out_shape=(out_sds, out_sds),
        grid=(B, d, hp, L // Lq),
        in_specs=[
            pl.BlockSpec((None, Lq, HEAD_DIM), lambda b, j, h, qi: (b, qi, zcol(0)(b, j, h, qi))),
            pl.BlockSpec((None, L, HEAD_DIM), lambda b, j, h, qi: (b, 0, zcol(1)(b, j, h, qi))),
            pl.BlockSpec((None, L, HEAD_DIM), lambda b, j, h, qi: (b, 0, zcol(2)(b, j, h, qi))),
            pl.BlockSpec((Lq, HEAD_DIM), lambda b, j, h, qi: (qi, j)),
            pl.BlockSpec((Lq, HEAD_DIM), lambda b, j, h, qi: (qi, j)),
            pl.BlockSpec((L, HEAD_DIM), lambda b, j, h, qi: (0, j)),
            pl.BlockSpec((L, HEAD_DIM), lambda b, j, h, qi: (0, j)),
            pl.BlockSpec((1, HEAD_DIM), lambda b, j, h, qi: (0, 0)),
            pl.BlockSpec((1, HEAD_DIM), lambda b, j, h, qi: (0, 0)),
        ],
        out_specs=(out_spec, out_spec),
        scratch_shapes=[pltpu.VMEM((L, HEAD_DIM), jnp.bfloat16)],
        compiler_params=pltpu.CompilerParams(
            dimension_semantics=("arbitrary", "arbitrary", "arbitrary", "arbitrary"),
            vmem_limit_bytes=VMEM_LIMIT_BYTES,
        ),
        name=f"dilated_attn_d{d}",
    )(zv, zv, zv, cosv, sinv, cosv, sinv, gq, gk)
    return o.reshape(B, T, hp * HEAD_DIM), lse.reshape(B, T, hp * HEAD_DIM)


def _dilated_mixer(z, qnorm_g, knorm_g):
    T = z.shape[1]
    cos2, sin2 = _rope_tables(T)
    outs, lses = [], []
    for g, (_, d) in enumerate(DIL_GROUPS):
        o, lse = _dilated_group(z, g, d, qnorm_g, knorm_g, cos2, sin2)
        outs.append(o)
        lses.append(lse)
    alpha = jax.nn.softmax(jnp.stack(lses, axis=0), axis=0)
    return jnp.concatenate([o * alpha[g] for g, o in enumerate(outs)], axis=-1)


def _rms_norm(x, g):
    x32 = x.astype(jnp.float32)
    y = x32 * lax.rsqrt(jnp.mean(x32 * x32, axis=-1, keepdims=True) + EPS) * g.astype(jnp.float32)
    return y.astype(x.dtype)


def _expert_choice_ffn(xf, w_router, w_gate, w_up, w_down):
    N, D = xf.shape
    aff = jax.nn.softmax((xf @ w_router).astype(jnp.float32), axis=-1)
    cap = (EC_CAPACITY_FACTOR * N) // N_EXPERTS
    gates, idx = lax.top_k(aff.T, cap)
    xe = xf[idx]
    hdn = jax.nn.silu(jnp.einsum('ecd,edf->ecf', xe, w_gate)) * jnp.einsum('ecd,edf->ecf', xe, w_up)
    ye = jnp.einsum('ecf,efd->ecd', hdn, w_down) * gates[..., None]
    return jnp.zeros((N, D), xf.dtype).at[idx.reshape(-1)].add(ye.reshape(-1, D))


def _layer(h, p_i, lb_f, lb_b, norm1_g, w_in, a_norm_g, b_qnorm_g, b_knorm_g, b_rpb,
           c_qnorm_g, c_knorm_g, w_out, norm2_g, w_router, w_gate, w_up, w_down,
           w_ple, ple_norm_g, gate_norm_g, w_ple_gate):
    B, T, _ = h.shape
    N = B * T
    hf = h.reshape(N, D_MODEL)
    z = _matmul(hf, w_in, norm1_g).reshape(B, T, IN_WIDTH)
    a_out = _hgrn2_mixer(z, lb_f, lb_b, a_norm_g)
    b_out = _neighborhood_mixer(z, b_rpb, b_qnorm_g, b_knorm_g)
    c_out = _dilated_mixer(z, c_qnorm_g, c_knorm_g)
    mix = jnp.concatenate([a_out, b_out, c_out], axis=-1).reshape(N, D_MODEL)
    hf = hf + _matmul(mix, w_out)
    hf = hf + _expert_choice_ffn(_rms_norm(hf, norm2_g), w_router, w_gate, w_up, w_down)
    e = _rms_norm(_matmul(p_i.reshape(N, -1), w_ple), ple_norm_g)
    gate = jax.nn.sigmoid(_matmul(hf, w_ple_gate, gate_norm_g))
    return (hf + gate * e).reshape(B, T, D_MODEL)


def _trunk(h, p, lb, weights):
    (norm1_g, w_in, a_norm_g, b_qnorm_g, b_knorm_g, b_rpb, c_qnorm_g, c_knorm_g, w_out,
     norm2_g, w_router, w_gate, w_up, w_down, w_ple, ple_norm_g, gate_norm_g, w_ple_gate) = weights
    for i in range(DEPTH):
        h = _layer(h, p[i], lb[0, i], lb[1, i], norm1_g[i], w_in[i], a_norm_g[i], b_qnorm_g[i],
                   b_knorm_g[i], b_rpb[i], c_qnorm_g[i], c_knorm_g[i], w_out[i], norm2_g[i],
                   w_router[i], w_gate[i], w_up[i], w_down[i], w_ple[i], ple_norm_g[i],
                   gate_norm_g[i], w_ple_gate[i])
    return h


def kernel(x_prompt, x_sample, p_prompt, p_sample, norm1_g, w_in, lb_logits, a_norm_g,
           b_qnorm_g, b_knorm_g, b_rpb, c_qnorm_g, c_knorm_g, w_out, norm2_g, w_router,
           w_gate, w_up, w_down, w_ple, ple_norm_g, gate_norm_g, w_ple_gate):
    pr = jax.nn.softmax(lb_logits.astype(jnp.float32), axis=1)
    lb = jnp.cumsum(pr, axis=1) - pr[:, :1]
    weights = (norm1_g, w_in, a_norm_g, b_qnorm_g, b_knorm_g, b_rpb, c_qnorm_g, c_knorm_g, w_out,
               norm2_g, w_router, w_gate, w_up, w_down, w_ple, ple_norm_g, gate_norm_g, w_ple_gate)
    y_prompt = _trunk(x_prompt, p_prompt, lb, weights)
    y_sample = _trunk(x_sample, p_sample, lb, weights)
    return (y_prompt, y_sample)
```

```python
import functools

import jax
import jax.numpy as jnp
import numpy as np
from jax import lax
from jax.experimental import pallas as pl
from jax.experimental.pallas import tpu as pltpu

D_MODEL = 2048
DEPTH = 4
HEAD_DIM = 128
A_HEADS = 6
B_HEADS = 4
C_HEADS = 6
A_WIDTH = A_HEADS * HEAD_DIM
B_WIDTH = B_HEADS * HEAD_DIM
C_WIDTH = C_HEADS * HEAD_DIM
IN_WIDTH = 5 * A_WIDTH + 3 * B_WIDTH + 3 * C_WIDTH
HGRN_CHUNK = 128
GRID_W = 64
NA_WIN_R = 8
NA_WIN_C = 16
DIL_GROUPS = ((128, 1), (512, 4), (2048, 16))
C_HEADS_PER_GROUP = C_HEADS // len(DIL_GROUPS)
ROPE_THETA = 10000.0
N_EXPERTS = 16
EC_CAPACITY_FACTOR = 2
EPS = 1e-6

VMEM_LIMIT_BYTES = 56 * 1024 * 1024
NEG_INF = -1e30
Z_COLS = IN_WIDTH // HEAD_DIM
B_COL0 = 5 * A_HEADS
C_COL0 = B_COL0 + 3 * B_HEADS


def _sigmoid(x):
    return 1.0 / (1.0 + jnp.exp(-x))


def _bf16(x):
    return x.astype(jnp.bfloat16)


def _dot(a, b):
    return jnp.dot(_bf16(a), _bf16(b), preferred_element_type=jnp.float32)


def _dot_nt(a, b):
    return lax.dot_general(_bf16(a), _bf16(b), (((1,), (1,)), ((), ())),
                           preferred_element_type=jnp.float32)


def _dot_tn(a, b):
    return lax.dot_general(_bf16(a), _bf16(b), (((0,), (0,)), ((), ())),
                           preferred_element_type=jnp.float32)


def _rms(x, g):
    return x * lax.rsqrt(jnp.mean(x * x, axis=-1, keepdims=True) + EPS) * g


def _norm_matmul_kernel(x_ref, g_ref, w_ref, o_ref, xn_ref, *, normalize):
    @pl.when(pl.program_id(1) == 0)
    def _():
        x = x_ref[...]
        if normalize:
            x = _rms(x, g_ref[...])
        xn_ref[...] = x.astype(jnp.bfloat16)

    o_ref[...] = jnp.dot(xn_ref[...], w_ref[...], preferred_element_type=jnp.float32)


def _matmul(x, w, gain=None, *, tm=1024, tn=512):
    n, k = x.shape
    m = w.shape[1]
    tm = min(tm, n)
    tn = min(tn, m)
    assert n % tm == 0 and m % tn == 0
    normalize = gain is not None
    g = (gain if normalize else jnp.ones((k,), jnp.float32)).reshape(1, k).astype(jnp.float32)
    return pl.pallas_call(
        functools.partial(_norm_matmul_kernel, normalize=normalize),
        out_shape=jax.ShapeDtypeStruct((n, m), jnp.float32),
        grid=(n // tm, m // tn),
        in_specs=[
            pl.BlockSpec((tm, k), lambda i, j: (i, 0)),
            pl.BlockSpec((1, k), lambda i, j: (0, 0)),
            pl.BlockSpec((k, tn), lambda i, j: (0, j)),
        ],
        out_specs=pl.BlockSpec((tm, tn), lambda i, j: (i, j)),
        scratch_shapes=[pltpu.VMEM((tm, k), jnp.bfloat16)],
        compiler_params=pltpu.CompilerParams(
            dimension_semantics=("arbitrary", "arbitrary"),
            vmem_limit_bytes=VMEM_LIMIT_BYTES,
        ),
        name="norm_matmul" if normalize else "matmul",
    )(x, g, w.astype(jnp.bfloat16))


HGRN_LEVELS = tuple(HGRN_CHUNK >> (i + 1) for i in range(HGRN_CHUNK.bit_length() - 1))
HGRN_NLEV = len(HGRN_LEVELS)
HGRN_HPS = 2
HGRN_COLS = HGRN_HPS * HEAD_DIM


def _hgrn_tables():
    C = HGRN_CHUNK
    t = np.arange(C)[:, None]
    u = np.arange(C)[None, :]
    dmats, amasks = [], []
    for reverse in (False, True):
        cum = ((u >= t) if reverse else (u <= t)).astype(np.float32)
        blocks, masks = [], []
        for m in HGRN_LEVELS:
            pos = np.arange(C) % (2 * m)
            ref = np.arange(C) - pos + (m if reverse else m - 1)
            blocks.append(cum - cum[ref])
            later = (pos < m) if reverse else (pos >= m)
            same = (t // (2 * m)) == (u // (2 * m))
            masks.append((same & later[:, None] & ~later[None, :]).astype(np.float32))
        last = 0 if reverse else C - 1
        blocks += [cum, cum[last:last + 1] - cum]
        masks.append(np.eye(C, dtype=np.float32))
        assert np.array_equal(sum(masks), cum)
        dmats.append(np.tile(np.concatenate(blocks, axis=0), (1, 2)))
        amasks.append(np.stack(masks, axis=0))
    return np.stack(dmats, axis=0), np.stack(amasks, axis=0)


def _hgrn_intra(q, x, v, lb, dmat, amask, reverse):
    C = HGRN_CHUNK
    f = lb + (1.0 - lb) * _sigmoid(x)
    lf = jnp.log(f)
    k = 1.0 - f
    hi = _bf16(lf)
    mid = _bf16(lf - hi.astype(jnp.float32))
    dall = jnp.dot(dmat[...], jnp.concatenate([hi, mid], axis=0),
                   preferred_element_type=jnp.float32)

    def stack(a):
        return jnp.concatenate([a[:, j * HEAD_DIM:(j + 1) * HEAD_DIM] for j in range(HGRN_HPS)], axis=0)

    def head_blocks(p):
        return [p[j * C:(j + 1) * C, j * C:(j + 1) * C] for j in range(HGRN_HPS)]

    qs, ks = stack(q), stack(k)
    row = lax.broadcasted_iota(jnp.int32, (HGRN_HPS * C, HEAD_DIM), 0)
    a = [amask[HGRN_NLEV] * p for p in head_blocks(_dot_nt(qs, ks))]
    for l, m in enumerate(HGRN_LEVELS):
        later = ((row & m) == 0) if reverse else ((row & m) != 0)
        xh = _bf16(jnp.where(later, qs, ks) * jnp.exp(-jnp.abs(stack(dall[l * C:(l + 1) * C]))))
        pairs = lax.dot_general(xh, xh, (((1,), (1,)), ((), ())), preferred_element_type=jnp.float32)
        a = [a_j + amask[l] * p for a_j, p in zip(a, head_blocks(pairs))]
    oi = jnp.concatenate([_dot(a[j], v[:, j * HEAD_DIM:(j + 1) * HEAD_DIM]) for j in range(HGRN_HPS)],
                         axis=1)
    b = dall[HGRN_NLEV * C:(HGRN_NLEV + 1) * C]
    b_rest = dall[(HGRN_NLEV + 1) * C:(HGRN_NLEV + 2) * C]
    b_last = b[0:1, :] if reverse else b[C - 1:C, :]
    return oi, q * jnp.exp(b), k * jnp.exp(b_rest), jnp.exp(b_last)


def _hgrn_kernel(q_ref, ff_ref, fb_ref, v_ref, g_ref, lb_ref, ng_ref, dmat_ref, amask_ref, o_ref,
                 ofwd_ref, st_ref, oi_ref, qt_ref, kt_ref, eb_ref, *, n_tiles, tile_rows):
    s = pl.program_id(2)
    n_chunks = tile_rows // HGRN_CHUNK
    head_cols = [slice(j * HEAD_DIM, (j + 1) * HEAD_DIM) for j in range(HGRN_HPS)]

    @pl.when((s == 0) | (s == n_tiles))
    def _():
        st_ref[...] = jnp.zeros_like(st_ref)

    def intra_pass(f_ref, direction, reverse):
        def body(c, carry):
            rows = pl.ds(pl.multiple_of(c * HGRN_CHUNK, HGRN_CHUNK), HGRN_CHUNK)
            qp = q_ref[rows, :]
            oi, qt, kt, eb = _hgrn_intra(qp * _sigmoid(qp), f_ref[rows, :], v_ref[rows, :],
                                         lb_ref[direction:direction + 1, :], dmat_ref, amask_ref, reverse)
            oi_ref[rows, :] = oi
            qt_ref[rows, :] = _bf16(qt)
            kt_ref[rows, :] = _bf16(kt)
            eb_ref[pl.ds(pl.multiple_of(c * 8, 8), 8), :] = jnp.broadcast_to(eb, (8, HGRN_COLS))
            return carry

        lax.fori_loop(0, n_chunks, body, 0)

    def state_step(c):
        rows = pl.ds(pl.multiple_of(c * HGRN_CHUNK, HGRN_CHUNK), HGRN_CHUNK)
        outs = []
        for j, cols in enumerate(head_cols):
            st = st_ref[j]
            outs.append(oi_ref[rows, cols] + _dot_nt(qt_ref[rows, cols], st))
            st_ref[j] = st * eb_ref[pl.ds(pl.multiple_of(c * 8, 8), 1), cols] + _dot_tn(v_ref[rows, cols],
                                                                                       kt_ref[rows, cols])
        return rows, outs

    @pl.when(s < n_tiles)
    def _():
        intra_pass(ff_ref, 0, False)
        base = s * tile_rows

        def body(c, carry):
            rows, outs = state_step(c)
            dst = pl.ds(pl.multiple_of(base + c * HGRN_CHUNK, HGRN_CHUNK), HGRN_CHUNK)
            for j, cols in enumerate(head_cols):
                ofwd_ref[dst, cols] = outs[j]
            return carry

        lax.fori_loop(0, n_chunks, body, 0, unroll=2)

    @pl.when(s >= n_tiles)
    def _():
        intra_pass(fb_ref, 1, True)
        base = (2 * n_tiles - 1 - s) * tile_rows

        def body(ci, carry):
            c = n_chunks - 1 - ci
            rows, outs = state_step(c)
            src = pl.ds(pl.multiple_of(base + c * HGRN_CHUNK, HGRN_CHUNK), HGRN_CHUNK)
            for j, cols in enumerate(head_cols):
                y = _rms(outs[j] + ofwd_ref[src, cols], ng_ref[...])
                g = g_ref[rows, cols]
                o_ref[rows, cols] = y * (g * _sigmoid(g))
            return carry

        lax.fori_loop(0, n_chunks, body, 0, unroll=2)


def _hgrn_tile_rows(T):
    return min(T, 2048)


def _hgrn2_mixer(z, lb_f, lb_b, norm_g):
    B, T, _ = z.shape
    tile_rows = _hgrn_tile_rows(T)
    n_tiles = T // tile_rows
    n_groups = A_HEADS // HGRN_HPS
    lb = jnp.stack([lb_f.reshape(n_groups, HGRN_COLS), lb_b.reshape(n_groups, HGRN_COLS)], axis=1)
    last = n_tiles - 1
    dmat, amask = _hgrn_tables()

    def both(s):
        return jnp.where(s < n_tiles, s, 2 * n_tiles - 1 - s)

    def fwd_only(s):
        return jnp.minimum(s, last)

    def bwd_only(s):
        return jnp.where(s < n_tiles, last, 2 * n_tiles - 1 - s)

    def zspec(part, tile_of):
        return pl.BlockSpec((None, tile_rows, HGRN_COLS),
                            lambda b, h, s: (b, tile_of(s), part * n_groups + h))

    return pl.pallas_call(
        functools.partial(_hgrn_kernel, n_tiles=n_tiles, tile_rows=tile_rows),
        out_shape=jax.ShapeDtypeStruct((B, T, A_WIDTH), jnp.float32),
        grid=(B, n_groups, 2 * n_tiles),
        in_specs=[
            zspec(0, both), zspec(1, fwd_only), zspec(2, bwd_only), zspec(3, both), zspec(4, bwd_only),
            pl.BlockSpec((None, 2, HGRN_COLS), lambda b, h, s: (h, 0, 0)),
            pl.BlockSpec((1, HEAD_DIM), lambda b, h, s: (0, 0)),
            pl.BlockSpec((None,) + dmat.shape[1:], lambda b, h, s: (s // n_tiles, 0, 0)),
            pl.BlockSpec((None,) + amask.shape[1:], lambda b, h, s: (s // n_tiles, 0, 0, 0)),
        ],
        out_specs=pl.BlockSpec((None, tile_rows, HGRN_COLS), lambda b, h, s: (b, bwd_only(s), h)),
        scratch_shapes=[pltpu.VMEM((T, HGRN_COLS), jnp.float32),
                        pltpu.VMEM((HGRN_HPS, HEAD_DIM, HEAD_DIM), jnp.float32),
                        pltpu.VMEM((tile_rows, HGRN_COLS), jnp.float32),
                        pltpu.VMEM((tile_rows, HGRN_COLS), jnp.bfloat16),
                        pltpu.VMEM((tile_rows, HGRN_COLS), jnp.bfloat16),
                        pltpu.VMEM((tile_rows // HGRN_CHUNK * 8, HGRN_COLS), jnp.float32)],
        compiler_params=pltpu.CompilerParams(
            dimension_semantics=("arbitrary", "arbitrary", "arbitrary"),
            vmem_limit_bytes=VMEM_LIMIT_BYTES,
        ),
        name="hgrn2",
    )(z, z, z, z, z, lb.astype(jnp.float32), norm_g.reshape(1, HEAD_DIM).astype(jnp.float32),
      jnp.asarray(dmat, jnp.bfloat16), jnp.asarray(amask, jnp.float32))


NA_KEYS = NA_WIN_R * GRID_W


def _na_bias_table(rpb):
    qc = np.arange(GRID_W)[:, None]
    kc = np.arange(GRID_W)[None, :]
    ws = np.clip(qc - NA_WIN_C // 2, 0, GRID_W - NA_WIN_C)
    col_ok = (kc >= ws) & (kc < ws + NA_WIN_C)
    dc = np.clip(kc - qc + NA_WIN_C - 1, 0, 2 * NA_WIN_C - 2)
    toe = jnp.where(jnp.asarray(col_ok)[None, None], rpb.astype(jnp.float32)[:, :, dc], NEG_INF)
    cases = []
    for c in range(NA_WIN_R):
        cases.append(jnp.concatenate([toe[:, c + j] for j in range(NA_WIN_R)], axis=-1))
    return jnp.stack(cases, axis=1)


def _na_kernel(q_ref, k_ref, v_ref, bias_ref, qg_ref, kg_ref, o_ref, *, rows, tile_grid_rows):
    qt = pl.program_id(2)

    def body(rl, carry):
        r = qt * tile_grid_rows + rl
        rs = jnp.clip(r - NA_WIN_R // 2, 0, rows - NA_WIN_R)
        q0 = pl.multiple_of(rl * GRID_W, GRID_W)
        k0 = pl.multiple_of(rs * GRID_W, GRID_W)
        q = _rms(q_ref[pl.ds(q0, GRID_W), :], qg_ref[...]) * (1.0 / np.sqrt(HEAD_DIM))
        k = _rms(k_ref[pl.ds(k0, NA_KEYS), :], kg_ref[...])
        s = _dot_nt(q, k) + bias_ref[rs - r + NA_WIN_R - 1]
        m = jnp.max(s, axis=-1, keepdims=True)
        p = jnp.exp(s - m)
        den = jnp.sum(p, axis=-1, keepdims=True)
        o_ref[pl.ds(q0, GRID_W), :] = _dot(p, v_ref[pl.ds(k0, NA_KEYS), :]) / den
        return carry

    lax.fori_loop(0, tile_grid_rows, body, 0, unroll=2)


def _neighborhood_mixer(z, rpb, qnorm_g, knorm_g):
    B, T, _ = z.shape
    rows = T // GRID_W
    assert rows >= NA_WIN_R
    tile_grid_rows = min(rows, 32)
    tq = tile_grid_rows * GRID_W
    bias = _na_bias_table(rpb)
    gq = qnorm_g.reshape(1, HEAD_DIM).astype(jnp.float32)
    gk = knorm_g.reshape(1, HEAD_DIM).astype(jnp.float32)
    return pl.pallas_call(
        functools.partial(_na_kernel, rows=rows, tile_grid_rows=tile_grid_rows),
        out_shape=jax.ShapeDtypeStruct((B, T, B_WIDTH), jnp.float32),
        grid=(B, B_HEADS, T // tq),
        in_specs=[
            pl.BlockSpec((None, tq, HEAD_DIM), lambda b, h, t: (b, t, B_COL0 + h)),
            pl.BlockSpec((None, T, HEAD_DIM), lambda b, h, t: (b, 0, B_COL0 + B_HEADS + h)),
            pl.BlockSpec((None, T, HEAD_DIM), lambda b, h, t: (b, 0, B_COL0 + 2 * B_HEADS + h)),
            pl.BlockSpec((None, NA_WIN_R, GRID_W, NA_KEYS), lambda b, h, t: (h, 0, 0, 0)),
            pl.BlockSpec((1, HEAD_DIM), lambda b, h, t: (0, 0)),
            pl.BlockSpec((1, HEAD_DIM), lambda b, h, t: (0, 0)),
        ],
        out_specs=pl.BlockSpec((None, tq, HEAD_DIM), lambda b, h, t: (b, t, h)),
        compiler_params=pltpu.CompilerParams(
            dimension_semantics=("arbitrary", "arbitrary", "arbitrary"),
            vmem_limit_bytes=VMEM_LIMIT_BYTES,
        ),
        name="neighborhood_attn",
    )(z, z, z, bias, gq, gk)


DIL_HALF = 64
DIL_QB = 128
assert all(w // (2 * d) == DIL_HALF for w, d in DIL_GROUPS)


def _rope_tables(T):
    half = HEAD_DIM // 2
    inv = 1.0 / (ROPE_THETA ** (jnp.arange(half, dtype=jnp.float32) * 2.0 / HEAD_DIM))
    ang = jnp.arange(T, dtype=jnp.float32)[:, None] * inv[None, :]
    cos, sin = jnp.cos(ang), jnp.sin(ang)
    return jnp.concatenate([cos, cos], axis=-1), jnp.concatenate([-sin, sin], axis=-1)


def _dil_kernel(q_ref, k_ref, v_ref, qcos_ref, qsin_ref, kcos_ref, ksin_ref, qg_ref, kg_ref,
                o_ref, lse_ref, kn_ref, *, L, Lq, KW, QB, KB):
    qi = pl.program_id(3)

    def norm_rope(x, g, cos, sin):
        xn = _rms(x, g)
        return xn * cos + pltpu.roll(xn, HEAD_DIM // 2, 1) * sin

    @pl.when(qi == 0)
    def _():
        def kbody(i, carry):
            r = pl.ds(pl.multiple_of(i * KB, KB), KB)
            kn_ref[r, :] = _bf16(norm_rope(k_ref[r, :], kg_ref[...], kcos_ref[r, :], ksin_ref[r, :]))
            return carry

        lax.fori_loop(0, L // KB, kbody, 0)

    def body(i, carry):
        q0l = pl.multiple_of(i * QB, QB)
        q0 = qi * Lq + q0l
        kb = pl.multiple_of(jnp.clip(q0 - DIL_HALF, 0, L - KW), DIL_HALF)
        rows = pl.ds(q0l, QB)
        q = norm_rope(q_ref[rows, :], qg_ref[...], qcos_ref[rows, :], qsin_ref[rows, :])
        s = _dot_nt(q * (1.0 / np.sqrt(HEAD_DIM)), kn_ref[pl.ds(kb, KW), :])
        qpos = q0 + lax.broadcasted_iota(jnp.int32, (QB, KW), 0)
        kpos = kb + lax.broadcasted_iota(jnp.int32, (QB, KW), 1)
        s = jnp.where(jnp.abs(qpos - kpos) <= DIL_HALF, s, NEG_INF)
        m = jnp.max(s, axis=-1, keepdims=True)
        p = jnp.exp(s - m)
        den = jnp.sum(p, axis=-1, keepdims=True)
        o_ref[rows, :] = _dot(p, v_ref[pl.ds(kb, KW), :]) / den
        lse_ref[rows, :] = jnp.broadcast_to(m + jnp.log(den), (QB, HEAD_DIM))
        return carry

    lax.fori_loop(0, Lq // QB, body, 0, unroll=2 if Lq // QB >= 2 else 1)


def _dilated_group(z, g, d, qnorm_g, knorm_g, cos2, sin2):
    B, T, _ = z.shape
    hp = C_HEADS_PER_GROUP
    L = T // d
    Lq = min(L, 1024)
    QB = min(DIL_QB, Lq)
    KW = min(2 * DIL_QB, L)
    KB = min(L, 512)
    zv = z.reshape(B, L, d * 3 * C_WIDTH)
    cosv = cos2.reshape(L, d * HEAD_DIM)
    sinv = sin2.reshape(L, d * HEAD_DIM)
    gq = qnorm_g.reshape(1, HEAD_DIM).astype(jnp.float32)
    gk = knorm_g.reshape(1, HEAD_DIM).astype(jnp.float32)

    def zcol(part):
        return lambda b, j, h, qi: j * 3 * C_HEADS + part * C_HEADS + g * hp + h

    out_sds = jax.ShapeDtypeStruct((B, L, d * hp * HEAD_DIM), jnp.float32)
    out_spec = pl.BlockSpec((None, Lq, HEAD_DIM), lambda b, j, h, qi: (b, qi, j * hp + h))
    o, lse = pl.pallas_call(
        functools.partial(_dil_kernel, L=L, Lq=Lq, KW=KW, QB=QB, KB=KB),
        out_shape=(out_sds, out_sds),
        grid=(B, d, hp, L // Lq),
        in_specs=[
            pl.BlockSpec((None, Lq, HEAD_DIM), lambda b, j, h, qi: (b, qi, zcol(0)(b, j, h, qi))),
            pl.BlockSpec((None, L, HEAD_DIM), lambda b, j, h, qi: (b, 0, zcol(1)(b, j, h, qi))),
            pl.BlockSpec((None, L, HEAD_DIM), lambda b, j, h, qi: (b, 0, zcol(2)(b, j, h, qi))),
            pl.BlockSpec((Lq, HEAD_DIM), lambda b, j, h, qi: (qi, j)),
            pl.BlockSpec((Lq, HEAD_DIM), lambda b, j, h, qi: (qi, j)),
            pl.BlockSpec((L, HEAD_DIM), lambda b, j, h, qi: (0, j)),
            pl.BlockSpec((L, HEAD_DIM), lambda b, j, h, qi: (0, j)),
            pl.BlockSpec((1, HEAD_DIM), lambda b, j, h, qi: (0, 0)),
            pl.BlockSpec((1, HEAD_DIM), lambda b, j, h, qi: (0, 0)),
        ],
        out_specs=(out_spec, out_spec),
        scratch_shapes=[pltpu.VMEM((L, HEAD_DIM), jnp.bfloat16)],
        compiler_params=pltpu.CompilerParams(
            dimension_semantics=("arbitrary", "arbitrary", "arbitrary", "arbitrary"),
            vmem_limit_bytes=VMEM_LIMIT_BYTES,
        ),
        name=f"dilated_attn_d{d}",
    )(zv, zv, zv, cosv, sinv, cosv, sinv, gq, gk)
    return o.reshape(B, T, hp * HEAD_DIM), lse.reshape(B, T, hp * HEAD_DIM)


def _dilated_mixer(z, qnorm_g, knorm_g):
    T = z.shape[1]
    cos2, sin2 = _rope_tables(T)
    outs, lses = [], []
    for g, (_, d) in enumerate(DIL_GROUPS):
        o, lse = _dilated_group(z, g, d, qnorm_g, knorm_g, cos2, sin2)
        outs.append(o)
        lses.append(lse)
    return outs, lses


def _out_proj_kernel(h_ref, a_ref, b_ref, o0_ref, o1_ref, o2_ref, l0_ref, l1_ref, l2_ref, w_ref,
                     out_ref, mix_ref):
    @pl.when(pl.program_id(1) == 0)
    def _():
        mix_ref[:, :A_WIDTH] = _bf16(a_ref[...])
        mix_ref[:, A_WIDTH:A_WIDTH + B_WIDTH] = _bf16(b_ref[...])
        lses = [l0_ref[...], l1_ref[...], l2_ref[...]]
        m = jnp.maximum(jnp.maximum(lses[0], lses[1]), lses[2])
        es = [jnp.exp(l - m) for l in lses]
        inv = 1.0 / (es[0] + es[1] + es[2])
        gw = C_HEADS_PER_GROUP * HEAD_DIM
        for g, o_ref in enumerate((o0_ref, o1_ref, o2_ref)):
            c0 = A_WIDTH + B_WIDTH + g * gw
            mix_ref[:, c0:c0 + gw] = _bf16(o_ref[...] * (es[g] * inv))

    out_ref[...] = h_ref[...] + jnp.dot(mix_ref[...], w_ref[...], preferred_element_type=jnp.float32)


def _out_proj(hf, a_out, b_out, c_outs, c_lses, w_out, *, tm=512, tn=512):
    n = hf.shape[0]
    gw = C_HEADS_PER_GROUP * HEAD_DIM
    row = lambda width: pl.BlockSpec((tm, width), lambda i, j: (i, 0))
    return pl.pallas_call(
        _out_proj_kernel,
        out_shape=jax.ShapeDtypeStruct((n, D_MODEL), jnp.float32),
        grid=(n // tm, D_MODEL // tn),
        in_specs=[pl.BlockSpec((tm, tn), lambda i, j: (i, j)), row(A_WIDTH), row(B_WIDTH)]
        + [row(gw)] * 6 + [pl.BlockSpec((D_MODEL, tn), lambda i, j: (0, j))],
        out_specs=pl.BlockSpec((tm, tn), lambda i, j: (i, j)),
        scratch_shapes=[pltpu.VMEM((tm, D_MODEL), jnp.bfloat16)],
        compiler_params=pltpu.CompilerParams(
            dimension_semantics=("arbitrary", "arbitrary"),
            vmem_limit_bytes=VMEM_LIMIT_BYTES,
        ),
        name="out_proj",
    )(hf, a_out.reshape(n, A_WIDTH), b_out.reshape(n, B_WIDTH),
      *[o.reshape(n, gw) for o in c_outs], *[l.reshape(n, gw) for l in c_lses],
      w_out.astype(jnp.bfloat16))


def _ple_kernel(hrow_ref, h_ref, p_ref, gg_ref, pg_ref, wp_ref, wg_ref, out_ref, xn_ref, e_ref, *, tn):
    j = pl.program_id(1)

    @pl.when(j == 0)
    def _():
        xn_ref[...] = _bf16(_rms(hrow_ref[...], gg_ref[...]))
        e = _rms(jnp.dot(_bf16(p_ref[...]), wp_ref[...], preferred_element_type=jnp.float32), pg_ref[...])
        for jj in range(D_MODEL // tn):
            e_ref[jj] = e[:, jj * tn:(jj + 1) * tn]

    gate = _sigmoid(jnp.dot(xn_ref[...], wg_ref[...], preferred_element_type=jnp.float32))
    out_ref[...] = h_ref[...] + gate * e_ref[j]


def _ple(hf, p, w_ple, ple_norm_g, gate_norm_g, w_ple_gate, *, tm=512, tn=512):
    n = hf.shape[0]
    pd = p.shape[1]
    vec = lambda g: g.reshape(1, D_MODEL).astype(jnp.float32)
    return pl.pallas_call(
        functools.partial(_ple_kernel, tn=tn),
        out_shape=jax.ShapeDtypeStruct((n, D_MODEL), jnp.float32),
        grid=(n // tm, D_MODEL // tn),
        in_specs=[
            pl.BlockSpec((tm, D_MODEL), lambda i, j: (i, 0)),
            pl.BlockSpec((tm, tn), lambda i, j: (i, j)),
            pl.BlockSpec((tm, pd), lambda i, j: (i, 0)),
            pl.BlockSpec((1, D_MODEL), lambda i, j: (0, 0)),
            pl.BlockSpec((1, D_MODEL), lambda i, j: (0, 0)),
            pl.BlockSpec((pd, D_MODEL), lambda i, j: (0, 0)),
            pl.BlockSpec((D_MODEL, tn), lambda i, j: (0, j)),
        ],
        out_specs=pl.BlockSpec((tm, tn), lambda i, j: (i, j)),
        scratch_shapes=[pltpu.VMEM((tm, D_MODEL), jnp.bfloat16),
                        pltpu.VMEM((D_MODEL // tn, tm, tn), jnp.float32)],
        compiler_params=pltpu.CompilerParams(
            dimension_semantics=("arbitrary", "arbitrary"),
            vmem_limit_bytes=VMEM_LIMIT_BYTES,
        ),
        name="ple",
    )(hf, hf, p, vec(gate_norm_g), vec(ple_norm_g), w_ple.astype(jnp.bfloat16),
      w_ple_gate.astype(jnp.bfloat16))


def _rms_norm(x, g):
    x32 = x.astype(jnp.float32)
    y = x32 * lax.rsqrt(jnp.mean(x32 * x32, axis=-1, keepdims=True) + EPS) * g.astype(jnp.float32)
    return y.astype(x.dtype)


def _expert_choice_ffn(xf, w_router, w_gate, w_up, w_down):
    N, D = xf.shape
    aff = jax.nn.softmax((xf @ w_router).astype(jnp.float32), axis=-1)
    cap = (EC_CAPACITY_FACTOR * N) // N_EXPERTS
    gates, idx = lax.top_k(aff.T, cap)
    xe = xf[idx]
    hdn = jax.nn.silu(jnp.einsum('ecd,edf->ecf', xe, w_gate)) * jnp.einsum('ecd,edf->ecf', xe, w_up)
    ye = jnp.einsum('ecf,efd->ecd', hdn, w_down) * gates[..., None]
    return jnp.zeros((N, D), xf.dtype).at[idx.reshape(-1)].add(ye.reshape(-1, D))


def _layer(h, p_i, lb_f, lb_b, norm1_g, w_in, a_norm_g, b_qnorm_g, b_knorm_g, b_rpb,
           c_qnorm_g, c_knorm_g, w_out, norm2_g, w_router, w_gate, w_up, w_down,
           w_ple, ple_norm_g, gate_norm_g, w_ple_gate):
    B, T, _ = h.shape
    N = B * T
    hf = h.reshape(N, D_MODEL)
    ab_width = IN_WIDTH - 3 * C_WIDTH
    z_ab = _matmul(hf, w_in[:, :ab_width], norm1_g, tn=768).reshape(B, T, ab_width)
    z_c = _matmul(hf, w_in[:, ab_width:], norm1_g, tn=768).reshape(B, T, 3 * C_WIDTH)
    a_out = _hgrn2_mixer(z_ab, lb_f, lb_b, a_norm_g)
    b_out = _neighborhood_mixer(z_ab, b_rpb, b_qnorm_g, b_knorm_g)
    c_outs, c_lses = _dilated_mixer(z_c, c_qnorm_g, c_knorm_g)
    hf = _out_proj(hf, a_out, b_out, c_outs, c_lses, w_out)
    hf = hf + _expert_choice_ffn(_rms_norm(hf, norm2_g), w_router, w_gate, w_up, w_down)
    return _ple(hf, p_i.reshape(N, -1), w_ple, ple_norm_g, gate_norm_g, w_ple_gate).reshape(B, T, D_MODEL)


def _trunk(h, p, lb, weights):
    (norm1_g, w_in, a_norm_g, b_qnorm_g, b_knorm_g, b_rpb, c_qnorm_g, c_knorm_g, w_out,
     norm2_g, w_router, w_gate, w_up, w_down, w_ple, ple_norm_g, gate_norm_g, w_ple_gate) = weights
    for i in range(DEPTH):
        h = _layer(h, p[i], lb[0, i], lb[1, i], norm1_g[i], w_in[i], a_norm_g[i], b_qnorm_g[i],
                   b_knorm_g[i], b_rpb[i], c_qnorm_g[i], c_knorm_g[i], w_out[i], norm2_g[i],
                   w_router[i], w_gate[i], w_up[i], w_down[i], w_ple[i], ple_norm_g[i],
                   gate_norm_g[i], w_ple_gate[i])
    return h


def kernel(x_prompt, x_sample, p_prompt, p_sample, norm1_g, w_in, lb_logits, a_norm_g,
           b_qnorm_g, b_knorm_g, b_rpb, c_qnorm_g, c_knorm_g, w_out, norm2_g, w_router,
           w_gate, w_up, w_down, w_ple, ple_norm_g, gate_norm_g, w_ple_gate):
    pr = jax.nn.softmax(lb_logits.astype(jnp.float32), axis=1)
    lb = jnp.cumsum(pr, axis=1) - pr[:, :1]
    weights = (norm1_g, w_in, a_norm_g, b_qnorm_g, b_knorm_g, b_rpb, c_qnorm_g, c_knorm_g, w_out,
               norm2_g, w_router, w_gate, w_up, w_down, w_ple, ple_norm_g, gate_norm_g, w_ple_gate)
    y_prompt = _trunk(x_prompt, p_prompt, lb, weights)
    y_sample = _trunk(x_sample, p_sample, lb, weights)
    return (y_prompt, y_sample)
```

```python
import functools

import jax
import jax.numpy as jnp
import numpy as np
from jax import lax
from jax.experimental import pallas as pl
from jax.experimental.pallas import tpu as pltpu

D_MODEL = 2048
DEPTH = 4
HEAD_DIM = 128
A_HEADS = 6
B_HEADS = 4
C_HEADS = 6
A_WIDTH = A_HEADS * HEAD_DIM
B_WIDTH = B_HEADS * HEAD_DIM
C_WIDTH = C_HEADS * HEAD_DIM
IN_WIDTH = 5 * A_WIDTH + 3 * B_WIDTH + 3 * C_WIDTH
HGRN_CHUNK = 128
GRID_W = 64
NA_WIN_R = 8
NA_WIN_C = 16
DIL_GROUPS = ((128, 1), (512, 4), (2048, 16))
C_HEADS_PER_GROUP = C_HEADS // len(DIL_GROUPS)
ROPE_THETA = 10000.0
N_EXPERTS = 16
EC_CAPACITY_FACTOR = 2
EPS = 1e-6

VMEM_LIMIT_BYTES = 56 * 1024 * 1024
NEG_INF = -1e30
Z_COLS = IN_WIDTH // HEAD_DIM
B_COL0 = 5 * A_HEADS
C_COL0 = B_COL0 + 3 * B_HEADS


def _sigmoid(x):
    return 1.0 / (1.0 + jnp.exp(-x))


def _bf16(x):
    return x.astype(jnp.bfloat16)


def _dot(a, b):
    return jnp.dot(_bf16(a), _bf16(b), preferred_element_type=jnp.float32)


def _dot_nt(a, b):
    return lax.dot_general(_bf16(a), _bf16(b), (((1,), (1,)), ((), ())),
                           preferred_element_type=jnp.float32)


def _dot_tn(a, b):
    return lax.dot_general(_bf16(a), _bf16(b), (((0,), (0,)), ((), ())),
                           preferred_element_type=jnp.float32)


def _rms(x, g):
    return x * lax.rsqrt(jnp.mean(x * x, axis=-1, keepdims=True) + EPS) * g


def _norm_matmul_kernel(x_ref, g_ref, w_ref, o_ref, xn_ref, *, normalize):
    @pl.when(pl.program_id(1) == 0)
    def _():
        x = x_ref[...]
        if normalize:
            x = _rms(x, g_ref[...])
        xn_ref[...] = x.astype(jnp.bfloat16)

    o_ref[...] = jnp.dot(xn_ref[...], w_ref[...], preferred_element_type=jnp.float32)


def _matmul(x, w, gain=None, *, tm=1024, tn=512):
    n, k = x.shape
    m = w.shape[1]
    tm = min(tm, n)
    tn = min(tn, m)
    assert n % tm == 0 and m % tn == 0
    normalize = gain is not None
    g = (gain if normalize else jnp.ones((k,), jnp.float32)).reshape(1, k).astype(jnp.float32)
    return pl.pallas_call(
        functools.partial(_norm_matmul_kernel, normalize=normalize),
        out_shape=jax.ShapeDtypeStruct((n, m), jnp.float32),
        grid=(n // tm, m // tn),
        in_specs=[
            pl.BlockSpec((tm, k), lambda i, j: (i, 0)),
            pl.BlockSpec((1, k), lambda i, j: (0, 0)),
            pl.BlockSpec((k, tn), lambda i, j: (0, j)),
        ],
        out_specs=pl.BlockSpec((tm, tn), lambda i, j: (i, j)),
        scratch_shapes=[pltpu.VMEM((tm, k), jnp.bfloat16)],
        compiler_params=pltpu.CompilerParams(
            dimension_semantics=("arbitrary", "arbitrary"),
            vmem_limit_bytes=VMEM_LIMIT_BYTES,
        ),
        name="norm_matmul" if normalize else "matmul",
    )(x, g, w.astype(jnp.bfloat16))


HGRN_LEVELS = tuple(HGRN_CHUNK >> (i + 1) for i in range(HGRN_CHUNK.bit_length() - 1))
HGRN_NLEV = len(HGRN_LEVELS)
HGRN_HPS = 2
HGRN_COLS = HGRN_HPS * HEAD_DIM


def _hgrn_tables():
    C = HGRN_CHUNK
    t = np.arange(C)[:, None]
    u = np.arange(C)[None, :]
    dmats, amasks = [], []
    for reverse in (False, True):
        cum = ((u >= t) if reverse else (u <= t)).astype(np.float32)
        blocks, masks = [], []
        for m in HGRN_LEVELS:
            pos = np.arange(C) % (2 * m)
            ref = np.arange(C) - pos + (m if reverse else m - 1)
            blocks.append(cum - cum[ref])
            later = (pos < m) if reverse else (pos >= m)
            same = (t // (2 * m)) == (u // (2 * m))
            masks.append((same & later[:, None] & ~later[None, :]).astype(np.float32))
        last = 0 if reverse else C - 1
        blocks += [cum, cum[last:last + 1] - cum]
        masks.append(np.eye(C, dtype=np.float32))
        assert np.array_equal(sum(masks), cum)
        dmats.append(np.tile(np.concatenate(blocks, axis=0), (1, 2)))
        amasks.append(np.stack(masks, axis=0))
    return np.stack(dmats, axis=0), np.stack(amasks, axis=0)


def _hgrn_intra(q, x, v, lb, dmat, amask, reverse):
    C = HGRN_CHUNK
    f = lb + (1.0 - lb) * _sigmoid(x)
    lf = jnp.log(f)
    k = 1.0 - f
    hi = _bf16(lf)
    mid = _bf16(lf - hi.astype(jnp.float32))
    dall = jnp.dot(dmat[...], jnp.concatenate([hi, mid], axis=0),
                   preferred_element_type=jnp.float32)

    def stack(a):
        return jnp.concatenate([a[:, j * HEAD_DIM:(j + 1) * HEAD_DIM] for j in range(HGRN_HPS)], axis=0)

    def head_blocks(p):
        return [p[j * C:(j + 1) * C, j * C:(j + 1) * C] for j in range(HGRN_HPS)]

    qs, ks = stack(q), stack(k)
    row = lax.broadcasted_iota(jnp.int32, (HGRN_HPS * C, HEAD_DIM), 0)
    a = [amask[HGRN_NLEV] * p for p in head_blocks(_dot_nt(qs, ks))]
    for l, m in enumerate(HGRN_LEVELS):
        later = ((row & m) == 0) if reverse else ((row & m) != 0)
        xh = _bf16(jnp.where(later, qs, ks) * jnp.exp(-jnp.abs(stack(dall[l * C:(l + 1) * C]))))
        pairs = lax.dot_general(xh, xh, (((1,), (1,)), ((), ())), preferred_element_type=jnp.float32)
        a = [a_j + amask[l] * p for a_j, p in zip(a, head_blocks(pairs))]
    oi = jnp.concatenate([_dot(a[j], v[:, j * HEAD_DIM:(j + 1) * HEAD_DIM]) for j in range(HGRN_HPS)],
                         axis=1)
    b = dall[HGRN_NLEV * C:(HGRN_NLEV + 1) * C]
    b_rest = dall[(HGRN_NLEV + 1) * C:(HGRN_NLEV + 2) * C]
    b_last = b[0:1, :] if reverse else b[C - 1:C, :]
    return oi, q * jnp.exp(b), k * jnp.exp(b_rest), jnp.exp(b_last)


def _hgrn_kernel(q_ref, ff_ref, fb_ref, v_ref, g_ref, lb_ref, ng_ref, dmat_ref, amask_ref, o_ref,
                 ofwd_ref, st_ref, oi_ref, qt_ref, kt_ref, eb_ref, *, n_tiles, tile_rows):
    s = pl.program_id(2)
    n_chunks = tile_rows // HGRN_CHUNK
    head_cols = [slice(j * HEAD_DIM, (j + 1) * HEAD_DIM) for j in range(HGRN_HPS)]

    @pl.when((s == 0) | (s == n_tiles))
    def _():
        st_ref[...] = jnp.zeros_like(st_ref)

    def intra_pass(f_ref, direction, reverse):
        def body(c, carry):
            rows = pl.ds(pl.multiple_of(c * HGRN_CHUNK, HGRN_CHUNK), HGRN_CHUNK)
            qp = q_ref[rows, :]
            oi, qt, kt, eb = _hgrn_intra(qp * _sigmoid(qp), f_ref[rows, :], v_ref[rows, :],
                                         lb_ref[direction:direction + 1, :], dmat_ref, amask_ref, reverse)
            oi_ref[rows, :] = oi
            qt_ref[rows, :] = _bf16(qt)
            kt_ref[rows, :] = _bf16(kt)
            eb_ref[pl.ds(pl.multiple_of(c * 8, 8), 8), :] = jnp.broadcast_to(eb, (8, HGRN_COLS))
            return carry

        lax.fori_loop(0, n_chunks, body, 0)

    def state_step(c):
        rows = pl.ds(pl.multiple_of(c * HGRN_CHUNK, HGRN_CHUNK), HGRN_CHUNK)
        outs = []
        for j, cols in enumerate(head_cols):
            st = st_ref[j]
            outs.append(oi_ref[rows, cols] + _dot_nt(qt_ref[rows, cols], st))
            st_ref[j] = st * eb_ref[pl.ds(pl.multiple_of(c * 8, 8), 1), cols] + _dot_tn(v_ref[rows, cols],
                                                                                       kt_ref[rows, cols])
        return rows, outs

    @pl.when(s < n_tiles)
    def _():
        intra_pass(ff_ref, 0, False)
        base = s * tile_rows

        def body(c, carry):
            rows, outs = state_step(c)
            dst = pl.ds(pl.multiple_of(base + c * HGRN_CHUNK, HGRN_CHUNK), HGRN_CHUNK)
            for j, cols in enumerate(head_cols):
                ofwd_ref[dst, cols] = outs[j]
            return carry

        lax.fori_loop(0, n_chunks, body, 0, unroll=2)

    @pl.when(s >= n_tiles)
    def _():
        intra_pass(fb_ref, 1, True)
        base = (2 * n_tiles - 1 - s) * tile_rows

        def body(ci, carry):
            c = n_chunks - 1 - ci
            rows, outs = state_step(c)
            src = pl.ds(pl.multiple_of(base + c * HGRN_CHUNK, HGRN_CHUNK), HGRN_CHUNK)
            for j, cols in enumerate(head_cols):
                y = _rms(outs[j] + ofwd_ref[src, cols], ng_ref[...])
                g = g_ref[rows, cols]
                o_ref[rows, cols] = y * (g * _sigmoid(g))
            return carry

        lax.fori_loop(0, n_chunks, body, 0, unroll=2)


def _hgrn_tile_rows(T):
    return min(T, 2048)


def _hgrn2_mixer(z, lb_f, lb_b, norm_g):
    B, T, _ = z.shape
    tile_rows = _hgrn_tile_rows(T)
    n_tiles = T // tile_rows
    n_groups = A_HEADS // HGRN_HPS
    lb = jnp.stack([lb_f.reshape(n_groups, HGRN_COLS), lb_b.reshape(n_groups, HGRN_COLS)], axis=1)
    last = n_tiles - 1
    dmat, amask = _hgrn_tables()

    def both(s):
        return jnp.where(s < n_tiles, s, 2 * n_tiles - 1 - s)

    def fwd_only(s):
        return jnp.minimum(s, last)

    def bwd_only(s):
        return jnp.where(s < n_tiles, last, 2 * n_tiles - 1 - s)

    def zspec(part, tile_of):
        return pl.BlockSpec((None, tile_rows, HGRN_COLS),
                            lambda b, h, s: (b, tile_of(s), part * n_groups + h))

    return pl.pallas_call(
        functools.partial(_hgrn_kernel, n_tiles=n_tiles, tile_rows=tile_rows),
        out_shape=jax.ShapeDtypeStruct((B, T, A_WIDTH), jnp.float32),
        grid=(B, n_groups, 2 * n_tiles),
        in_specs=[
            zspec(0, both), zspec(1, fwd_only), zspec(2, bwd_only), zspec(3, both), zspec(4, bwd_only),
            pl.BlockSpec((None, 2, HGRN_COLS), lambda b, h, s: (h, 0, 0)),
            pl.BlockSpec((1, HEAD_DIM), lambda b, h, s: (0, 0)),
            pl.BlockSpec((None,) + dmat.shape[1:], lambda b, h, s: (s // n_tiles, 0, 0)),
            pl.BlockSpec((None,) + amask.shape[1:], lambda b, h, s: (s // n_tiles, 0, 0, 0)),
        ],
        out_specs=pl.BlockSpec((None, tile_rows, HGRN_COLS), lambda b, h, s: (b, bwd_only(s), h)),
        scratch_shapes=[pltpu.VMEM((T, HGRN_COLS), jnp.float32),
                        pltpu.VMEM((HGRN_HPS, HEAD_DIM, HEAD_DIM), jnp.float32),
                        pltpu.VMEM((tile_rows, HGRN_COLS), jnp.float32),
                        pltpu.VMEM((tile_rows, HGRN_COLS), jnp.bfloat16),
                        pltpu.VMEM((tile_rows, HGRN_COLS), jnp.bfloat16),
                        pltpu.VMEM((tile_rows // HGRN_CHUNK * 8, HGRN_COLS), jnp.float32)],
        compiler_params=pltpu.CompilerParams(
            dimension_semantics=("arbitrary", "arbitrary", "arbitrary"),
            vmem_limit_bytes=VMEM_LIMIT_BYTES,
        ),
        name="hgrn2",
    )(z, z, z, z, z, lb.astype(jnp.float32), norm_g.reshape(1, HEAD_DIM).astype(jnp.float32),
      jnp.asarray(dmat, jnp.bfloat16), jnp.asarray(amask, jnp.float32))


NA_KEYS = NA_WIN_R * GRID_W


def _na_bias_table(rpb):
    qc = np.arange(GRID_W)[:, None]
    kc = np.arange(GRID_W)[None, :]
    ws = np.clip(qc - NA_WIN_C // 2, 0, GRID_W - NA_WIN_C)
    col_ok = (kc >= ws) & (kc < ws + NA_WIN_C)
    dc = np.clip(kc - qc + NA_WIN_C - 1, 0, 2 * NA_WIN_C - 2)
    toe = jnp.where(jnp.asarray(col_ok)[None, None], rpb.astype(jnp.float32)[:, :, dc], NEG_INF)
    cases = []
    for c in range(NA_WIN_R):
        cases.append(jnp.concatenate([toe[:, c + j] for j in range(NA_WIN_R)], axis=-1))
    return jnp.stack(cases, axis=1)


def _na_kernel(q_ref, k_ref, v_ref, bias_ref, qg_ref, kg_ref, o_ref, *, rows, tile_grid_rows):
    qt = pl.program_id(2)

    def body(rl, carry):
        r = qt * tile_grid_rows + rl
        rs = jnp.clip(r - NA_WIN_R // 2, 0, rows - NA_WIN_R)
        q0 = pl.multiple_of(rl * GRID_W, GRID_W)
        k0 = pl.multiple_of(rs * GRID_W, GRID_W)
        q = _rms(q_ref[pl.ds(q0, GRID_W), :], qg_ref[...]) * (1.0 / np.sqrt(HEAD_DIM))
        k = _rms(k_ref[pl.ds(k0, NA_KEYS), :], kg_ref[...])
        s = _dot_nt(q, k) + bias_ref[rs - r + NA_WIN_R - 1]
        m = jnp.max(s, axis=-1, keepdims=True)
        p = jnp.exp(s - m)
        den = jnp.sum(p, axis=-1, keepdims=True)
        o_ref[pl.ds(q0, GRID_W), :] = _dot(p, v_ref[pl.ds(k0, NA_KEYS), :]) / den
        return carry

    lax.fori_loop(0, tile_grid_rows, body, 0, unroll=2)


def _neighborhood_mixer(z, rpb, qnorm_g, knorm_g):
    B, T, _ = z.shape
    rows = T // GRID_W
    assert rows >= NA_WIN_R
    tile_grid_rows = min(rows, 32)
    tq = tile_grid_rows * GRID_W
    bias = _na_bias_table(rpb)
    gq = qnorm_g.reshape(1, HEAD_DIM).astype(jnp.float32)
    gk = knorm_g.reshape(1, HEAD_DIM).astype(jnp.float32)
    return pl.pallas_call(
        functools.partial(_na_kernel, rows=rows, tile_grid_rows=tile_grid_rows),
        out_shape=jax.ShapeDtypeStruct((B, T, B_WIDTH), jnp.float32),
        grid=(B, B_HEADS, T // tq),
        in_specs=[
            pl.BlockSpec((None, tq, HEAD_DIM), lambda b, h, t: (b, t, B_COL0 + h)),
            pl.BlockSpec((None, T, HEAD_DIM), lambda b, h, t: (b, 0, B_COL0 + B_HEADS + h)),
            pl.BlockSpec((None, T, HEAD_DIM), lambda b, h, t: (b, 0, B_COL0 + 2 * B_HEADS + h)),
            pl.BlockSpec((None, NA_WIN_R, GRID_W, NA_KEYS), lambda b, h, t: (h, 0, 0, 0)),
            pl.BlockSpec((1, HEAD_DIM), lambda b, h, t: (0, 0)),
            pl.BlockSpec((1, HEAD_DIM), lambda b, h, t: (0, 0)),
        ],
        out_specs=pl.BlockSpec((None, tq, HEAD_DIM), lambda b, h, t: (b, t, h)),
        compiler_params=pltpu.CompilerParams(
            dimension_semantics=("arbitrary", "arbitrary", "arbitrary"),
            vmem_limit_bytes=VMEM_LIMIT_BYTES,
        ),
        name="neighborhood_attn",
    )(z, z, z, bias, gq, gk)


DIL_HALF = 64
DIL_QB = 128
assert all(w // (2 * d) == DIL_HALF for w, d in DIL_GROUPS)


def _rope_tables(T):
    half = HEAD_DIM // 2
    inv = 1.0 / (ROPE_THETA ** (jnp.arange(half, dtype=jnp.float32) * 2.0 / HEAD_DIM))
    ang = jnp.arange(T, dtype=jnp.float32)[:, None] * inv[None, :]
    cos, sin = jnp.cos(ang), jnp.sin(ang)
    return jnp.concatenate([cos, cos], axis=-1), jnp.concatenate([-sin, sin], axis=-1)


def _dil_kernel(q_ref, k_ref, v_ref, qcos_ref, qsin_ref, kcos_ref, ksin_ref, qg_ref, kg_ref,
                o_ref, lse_ref, kn_ref, *, L, Lq, KW, QB, KB):
    qi = pl.program_id(3)

    def norm_rope(x, g, cos, sin):
        xn = _rms(x, g)
        return xn * cos + pltpu.roll(xn, HEAD_DIM // 2, 1) * sin

    @pl.when(qi == 0)
    def _():
        def kbody(i, carry):
            r = pl.ds(pl.multiple_of(i * KB, KB), KB)
            kn_ref[r, :] = _bf16(norm_rope(k_ref[r, :], kg_ref[...], kcos_ref[r, :], ksin_ref[r, :]))
            return carry

        lax.fori_loop(0, L // KB, kbody, 0)

    def body(i, carry):
        q0l = pl.multiple_of(i * QB, QB)
        q0 = qi * Lq + q0l
        kb = pl.multiple_of(jnp.clip(q0 - DIL_HALF, 0, L - KW), DIL_HALF)
        rows = pl.ds(q0l, QB)
        q = norm_rope(q_ref[rows, :], qg_ref[...], qcos_ref[rows, :], qsin_ref[rows, :])
        s = _dot_nt(q * (1.0 / np.sqrt(HEAD_DIM)), kn_ref[pl.ds(kb, KW), :])
        qpos = q0 + lax.broadcasted_iota(jnp.int32, (QB, KW), 0)
        kpos = kb + lax.broadcasted_iota(jnp.int32, (QB, KW), 1)
        s = jnp.where(jnp.abs(qpos - kpos) <= DIL_HALF, s, NEG_INF)
        m = jnp.max(s, axis=-1, keepdims=True)
        p = jnp.exp(s - m)
        den = jnp.sum(p, axis=-1, keepdims=True)
        o_ref[rows, :] = _dot(p, v_ref[pl.ds(kb, KW), :]) / den
        lse_ref[rows, :] = jnp.broadcast_to(m + jnp.log(den), (QB, HEAD_DIM))
        return carry

    lax.fori_loop(0, Lq // QB, body, 0, unroll=2 if Lq // QB >= 2 else 1)


def _dilated_group(z, g, d, qnorm_g, knorm_g, cos2, sin2):
    B, T, _ = z.shape
    hp = C_HEADS_PER_GROUP
    L = T // d
    Lq = min(L, 1024)
    QB = min(DIL_QB, Lq)
    KW = min(2 * DIL_QB, L)
    KB = min(L, 512)
    zv = z.reshape(B, L, d * 3 * C_WIDTH)
    cosv = cos2.reshape(L, d * HEAD_DIM)
    sinv = sin2.reshape(L, d * HEAD_DIM)
    gq = qnorm_g.reshape(1, HEAD_DIM).astype(jnp.float32)
    gk = knorm_g.reshape(1, HEAD_DIM).astype(jnp.float32)

    def zcol(part):
        return lambda b, j, h, qi: j * 3 * C_HEADS + part * C_HEADS + g * hp + h

    out_sds = jax.ShapeDtypeStruct((B, L, d * hp * HEAD_DIM), jnp.float32)
    out_spec = pl.BlockSpec((None, Lq, HEAD_DIM), lambda b, j, h, qi: (b, qi, j * hp + h))
    o, lse = pl.pallas_call(
        functools.partial(_dil_kernel, L=L, Lq=Lq, KW=KW, QB=QB, KB=KB),
        out_shape=(out_sds, out_sds),
        grid=(B, d, hp, L // Lq),
        in_specs=[
            pl.BlockSpec((None, Lq, HEAD_DIM), lambda b, j, h, qi: (b, qi, zcol(0)(b, j, h, qi))),
            pl.BlockSpec((None, L, HEAD_DIM), lambda b, j, h, qi: (b, 0, zcol(1)(b, j, h, qi))),
            pl.BlockSpec((None, L, HEAD_DIM), lambda b, j, h, qi: (b, 0, zcol(2)(b, j, h, qi))),
            pl.BlockSpec((Lq, HEAD_DIM), lambda b, j, h, qi: (qi, j)),
            pl.BlockSpec((Lq, HEAD_DIM), lambda b, j, h, qi: (qi, j)),
            pl.BlockSpec((L, HEAD_DIM), lambda b, j, h, qi: (0, j)),
            pl.BlockSpec((L, HEAD_DIM), lambda b, j, h, qi: (0, j)),
            pl.BlockSpec((1, HEAD_DIM), lambda b, j, h, qi: (0, 0)),
            pl.BlockSpec((1, HEAD_DIM), lambda b, j, h, qi: (0, 0)),
        ],
        out_specs=(out_spec, out_spec),
        scratch_shapes=[pltpu.VMEM((L, HEAD_DIM), jnp.bfloat16)],
        compiler_params=pltpu.CompilerParams(
            dimension_semantics=("arbitrary", "arbitrary", "arbitrary", "arbitrary"),
            vmem_limit_bytes=VMEM_LIMIT_BYTES,
        ),
        name=f"dilated_attn_d{d}",
    )(zv, zv, zv, cosv, sinv, cosv, sinv, gq, gk)
    return o.reshape(B, T, hp * HEAD_DIM), lse.reshape(B, T, hp * HEAD_DIM)


def _dilated_mixer(z, qnorm_g, knorm_g):
    T = z.shape[1]
    cos2, sin2 = _rope_tables(T)
    outs, lses = [], []
    for g, (_, d) in enumerate(DIL_GROUPS):
        o, lse = _dilated_group(z, g, d, qnorm_g, knorm_g, cos2, sin2)
        outs.append(o)
        lses.append(lse)
    return outs, lses


def _out_proj_kernel(h_ref, a_ref, b_ref, o0_ref, o1_ref, o2_ref, l0_ref, l1_ref, l2_ref, w_ref,
                     out_ref, mix_ref):
    @pl.when(pl.program_id(1) == 0)
    def _():
        mix_ref[:, :A_WIDTH] = _bf16(a_ref[...])
        mix_ref[:, A_WIDTH:A_WIDTH + B_WIDTH] = _bf16(b_ref[...])
        lses = [l0_ref[...], l1_ref[...], l2_ref[...]]
        m = jnp.maximum(jnp.maximum(lses[0], lses[1]), lses[2])
        es = [jnp.exp(l - m) for l in lses]
        inv = 1.0 / (es[0] + es[1] + es[2])
        gw = C_HEADS_PER_GROUP * HEAD_DIM
        for g, o_ref in enumerate((o0_ref, o1_ref, o2_ref)):
            c0 = A_WIDTH + B_WIDTH + g * gw
            mix_ref[:, c0:c0 + gw] = _bf16(o_ref[...] * (es[g] * inv))

    out_ref[...] = h_ref[...] + jnp.dot(mix_ref[...], w_ref[...], preferred_element_type=jnp.float32)


def _out_proj(hf, a_out, b_out, c_outs, c_lses, w_out, *, tm=512, tn=512):
    n = hf.shape[0]
    gw = C_HEADS_PER_GROUP * HEAD_DIM
    row = lambda width: pl.BlockSpec((tm, width), lambda i, j: (i, 0))
    return pl.pallas_call(
        _out_proj_kernel,
        out_shape=jax.ShapeDtypeStruct((n, D_MODEL), jnp.float32),
        grid=(n // tm, D_MODEL // tn),
        in_specs=[pl.BlockSpec((tm, tn), lambda i, j: (i, j)), row(A_WIDTH), row(B_WIDTH)]
        + [row(gw)] * 6 + [pl.BlockSpec((D_MODEL, tn), lambda i, j: (0, j))],
        out_specs=pl.BlockSpec((tm, tn), lambda i, j: (i, j)),
        scratch_shapes=[pltpu.VMEM((tm, D_MODEL), jnp.bfloat16)],
        compiler_params=pltpu.CompilerParams(
            dimension_semantics=("arbitrary", "arbitrary"),
            vmem_limit_bytes=VMEM_LIMIT_BYTES,
        ),
        name="out_proj",
    )(hf, a_out.reshape(n, A_WIDTH), b_out.reshape(n, B_WIDTH),
      *[o.reshape(n, gw) for o in c_outs], *[l.reshape(n, gw) for l in c_lses],
      w_out.astype(jnp.bfloat16))


def _ple_kernel(hrow_ref, h_ref, p_ref, gg_ref, pg_ref, wp_ref, wg_ref, out_ref, xn_ref, e_ref, *, tn):
    j = pl.program_id(1)

    @pl.when(j == 0)
    def _():
        xn_ref[...] = _bf16(_rms(hrow_ref[...], gg_ref[...]))
        e = _rms(jnp.dot(_bf16(p_ref[...]), wp_ref[...], preferred_element_type=jnp.float32), pg_ref[...])
        for jj in range(D_MODEL // tn):
            e_ref[jj] = e[:, jj * tn:(jj + 1) * tn]

    gate = _sigmoid(jnp.dot(xn_ref[...], wg_ref[...], preferred_element_type=jnp.float32))
    out_ref[...] = h_ref[...] + gate * e_ref[j]


def _ple(hf, p, w_ple, ple_norm_g, gate_norm_g, w_ple_gate, *, tm=512, tn=512):
    n = hf.shape[0]
    pd = p.shape[1]
    vec = lambda g: g.reshape(1, D_MODEL).astype(jnp.float32)
    return pl.pallas_call(
        functools.partial(_ple_kernel, tn=tn),
        out_shape=jax.ShapeDtypeStruct((n, D_MODEL), jnp.float32),
        grid=(n // tm, D_MODEL // tn),
        in_specs=[
            pl.BlockSpec((tm, D_MODEL), lambda i, j: (i, 0)),
            pl.BlockSpec((tm, tn), lambda i, j: (i, j)),
            pl.BlockSpec((tm, pd), lambda i, j: (i, 0)),
            pl.BlockSpec((1, D_MODEL), lambda i, j: (0, 0)),
            pl.BlockSpec((1, D_MODEL), lambda i, j: (0, 0)),
            pl.BlockSpec((pd, D_MODEL), lambda i, j: (0, 0)),
            pl.BlockSpec((D_MODEL, tn), lambda i, j: (0, j)),
        ],
        out_specs=pl.BlockSpec((tm, tn), lambda i, j: (i, j)),
        scratch_shapes=[pltpu.VMEM((tm, D_MODEL), jnp.bfloat16),
                        pltpu.VMEM((D_MODEL // tn, tm, tn), jnp.float32)],
        compiler_params=pltpu.CompilerParams(
            dimension_semantics=("arbitrary", "arbitrary"),
            vmem_limit_bytes=VMEM_LIMIT_BYTES,
        ),
        name="ple",
    )(hf, hf, p, vec(gate_norm_g), vec(ple_norm_g), w_ple.astype(jnp.bfloat16),
      w_ple_gate.astype(jnp.bfloat16))


LANES = 128
FFN_ROWS = 512
COMBINE_ROWS = 256


def _router_kernel(h_ref, g_ref, wr_ref, xn_ref, aff_ref):
    xn = _rms(h_ref[...], g_ref[...])
    xn_ref[...] = xn
    logits = jnp.dot(xn, wr_ref[...], preferred_element_type=jnp.float32, precision=lax.Precision.HIGHEST)
    lane = lax.broadcasted_iota(jnp.int32, logits.shape, 1)
    logits = jnp.where(lane < N_EXPERTS, logits, NEG_INF)
    e = jnp.exp(logits - jnp.max(logits, axis=-1, keepdims=True))
    aff_ref[...] = e / jnp.sum(e, axis=-1, keepdims=True)


def _router(hf, norm_g, w_router, *, tm=512):
    n = hf.shape[0]
    wr = jnp.zeros((D_MODEL, LANES), jnp.float32).at[:, :N_EXPERTS].set(w_router.astype(jnp.float32))
    return pl.pallas_call(
        _router_kernel,
        out_shape=(jax.ShapeDtypeStruct((n, D_MODEL), jnp.float32),
                   jax.ShapeDtypeStruct((n, LANES), jnp.float32)),
        grid=(n // tm,),
        in_specs=[pl.BlockSpec((tm, D_MODEL), lambda i: (i, 0)),
                  pl.BlockSpec((1, D_MODEL), lambda i: (0, 0)),
                  pl.BlockSpec((D_MODEL, LANES), lambda i: (0, 0))],
        out_specs=(pl.BlockSpec((tm, D_MODEL), lambda i: (i, 0)),
                   pl.BlockSpec((tm, LANES), lambda i: (i, 0))),
        compiler_params=pltpu.CompilerParams(dimension_semantics=("arbitrary",),
                                             vmem_limit_bytes=VMEM_LIMIT_BYTES),
        name="router",
    )(hf, norm_g.reshape(1, D_MODEL).astype(jnp.float32), wr)


def _expert_kernel(dest_hbm, x_ref, g_ref, wg_ref, wu_ref, wd_ref, y_hbm, ybuf, dbuf, ysem, dsem,
                   *, n_steps, steps_per_expert):
    step = pl.program_id(0) * steps_per_expert + pl.program_id(1)
    slot = step % 2

    def dest_copy(s, sl):
        return pltpu.make_async_copy(dest_hbm.at[s], dbuf.at[sl], dsem.at[sl])

    def scatter_wait(sl):
        pltpu.make_async_copy(ybuf.at[sl], ybuf.at[sl], ysem.at[sl]).wait()

    @pl.when(step == 0)
    def _():
        dest_copy(0, 0).start()

    @pl.when(step + 1 < n_steps)
    def _():
        dest_copy(step + 1, 1 - slot).start()

    @pl.when(step >= 2)
    def _():
        scatter_wait(slot)

    x = _bf16(x_ref[...])
    gate = jnp.dot(x, wg_ref[...], preferred_element_type=jnp.float32)
    hdn = gate * _sigmoid(gate) * jnp.dot(x, wu_ref[...], preferred_element_type=jnp.float32)
    y = jnp.dot(_bf16(hdn), wd_ref[...], preferred_element_type=jnp.float32)
    ybuf[slot] = y * jnp.concatenate([g_ref[...]] * (D_MODEL // LANES), axis=1)

    dest_copy(step, slot).wait()
    for r in range(FFN_ROWS):
        pltpu.make_async_copy(ybuf.at[slot, pl.ds(r, 1), :],
                              y_hbm.at[pl.ds(dbuf[slot, r], 1), :], ysem.at[slot]).start()

    @pl.when(step == n_steps - 1)
    def _():
        scatter_wait(slot)
        if n_steps >= 2:
            scatter_wait(1 - slot)


def _expert_ffn(xe, gates_rep, dest, w_gate, w_up, w_down, n_tokens):
    rows, d = xe.shape
    cap = rows // N_EXPERTS
    f = w_gate.shape[-1]
    spe = cap // FFN_ROWS
    n_steps = N_EXPERTS * spe
    return pl.pallas_call(
        functools.partial(_expert_kernel, n_steps=n_steps, steps_per_expert=spe),
        out_shape=jax.ShapeDtypeStruct((N_EXPERTS * n_tokens, d), jnp.float32),
        grid=(N_EXPERTS, spe),
        in_specs=[
            pl.BlockSpec(memory_space=pl.ANY),
            pl.BlockSpec((FFN_ROWS, d), lambda e, t: (e * spe + t, 0)),
            pl.BlockSpec((FFN_ROWS, LANES), lambda e, t: (e * spe + t, 0)),
            pl.BlockSpec((None, d, f), lambda e, t: (e, 0, 0)),
            pl.BlockSpec((None, d, f), lambda e, t: (e, 0, 0)),
            pl.BlockSpec((None, f, d), lambda e, t: (e, 0, 0)),
        ],
        out_specs=pl.BlockSpec(memory_space=pl.ANY),
        scratch_shapes=[pltpu.VMEM((2, FFN_ROWS, d), jnp.float32),
                        pltpu.SMEM((2, FFN_ROWS), jnp.int32),
                        pltpu.SemaphoreType.DMA((2,)),
                        pltpu.SemaphoreType.DMA((2,))],
        compiler_params=pltpu.CompilerParams(dimension_semantics=("arbitrary", "arbitrary"),
                                             vmem_limit_bytes=VMEM_LIMIT_BYTES),
        name="expert_ffn",
    )(dest.reshape(n_steps, FFN_ROWS), xe, gates_rep, w_gate.astype(jnp.bfloat16),
      w_up.astype(jnp.bfloat16), w_down.astype(jnp.bfloat16))


def _combine_kernel(tmax_ref, h_ref, cnt_ref, y_ref, o_ref):
    i, r = pl.program_id(0), pl.program_id(1)

    @pl.when(r == 0)
    def _():
        o_ref[...] = h_ref[...]

    @pl.when(r < tmax_ref[i])
    def _():
        chosen = jnp.concatenate([cnt_ref[...]] * (D_MODEL // LANES), axis=1) > r
        o_ref[...] += jnp.where(chosen, y_ref[...], 0.0)


def _combine(hf, slabs, cnt_rep, tile_max):
    n, d = hf.shape
    tm = COMBINE_ROWS

    def slab_index(i, r, tmax):
        return (jnp.minimum(r, jnp.maximum(tmax[i], 1) - 1), i, 0)

    return pl.pallas_call(
        _combine_kernel,
        out_shape=jax.ShapeDtypeStruct((n, d), jnp.float32),
        grid_spec=pltpu.PrefetchScalarGridSpec(
            num_scalar_prefetch=1,
            grid=(n // tm, N_EXPERTS),
            in_specs=[pl.BlockSpec((tm, d), lambda i, r, tmax: (i, 0)),
                      pl.BlockSpec((tm, LANES), lambda i, r, tmax: (i, 0)),
                      pl.BlockSpec((None, tm, d), slab_index)],
            out_specs=pl.BlockSpec((tm, d), lambda i, r, tmax: (i, 0)),
        ),
        compiler_params=pltpu.CompilerParams(dimension_semantics=("arbitrary", "arbitrary"),
                                             vmem_limit_bytes=VMEM_LIMIT_BYTES),
        name="ffn_combine",
    )(tile_max, hf, cnt_rep, slabs.reshape(N_EXPERTS, n, d))


def _expert_choice_ffn(hf, norm_g, w_router, w_gate, w_up, w_down):
    n = hf.shape[0]
    cap = (EC_CAPACITY_FACTOR * n) // N_EXPERTS
    xn, aff = _router(hf, norm_g, w_router)
    gates, idx = lax.top_k(aff[:, :N_EXPERTS].T, cap)
    xe = xn[idx.reshape(-1)]
    chosen = jnp.zeros((N_EXPERTS, n), jnp.int32).at[jnp.arange(N_EXPERTS)[:, None], idx].set(1)
    rank = jnp.cumsum(chosen, axis=0) - chosen
    dest = jnp.take_along_axis(rank, idx, axis=1) * n + idx
    cnt = jnp.sum(chosen, axis=0)
    slabs = _expert_ffn(xe, jnp.broadcast_to(gates.reshape(-1, 1), (N_EXPERTS * cap, LANES)),
                        dest.reshape(-1).astype(jnp.int32), w_gate, w_up, w_down, n)
    cnt_rep = jnp.broadcast_to(cnt[:, None], (n, LANES)).astype(jnp.int32)
    tile_max = jnp.max(cnt.reshape(n // COMBINE_ROWS, COMBINE_ROWS), axis=1).astype(jnp.int32)
    return _combine(hf, slabs, cnt_rep, tile_max)


def _layer(h, p_i, lb_f, lb_b, norm1_g, w_in, a_norm_g, b_qnorm_g, b_knorm_g, b_rpb,
           c_qnorm_g, c_knorm_g, w_out, norm2_g, w_router, w_gate, w_up, w_down,
           w_ple, ple_norm_g, gate_norm_g, w_ple_gate):
    B, T, _ = h.shape
    N = B * T
    hf = h.reshape(N, D_MODEL)
    ab_width = IN_WIDTH - 3 * C_WIDTH
    z_ab = _matmul(hf, w_in[:, :ab_width], norm1_g, tn=768).reshape(B, T, ab_width)
    z_c = _matmul(hf, w_in[:, ab_width:], norm1_g, tn=768).reshape(B, T, 3 * C_WIDTH)
    a_out = _hgrn2_mixer(z_ab, lb_f, lb_b, a_norm_g)
    b_out = _neighborhood_mixer(z_ab, b_rpb, b_qnorm_g, b_knorm_g)
    c_outs, c_lses = _dilated_mixer(z_c, c_qnorm_g, c_knorm_g)
    hf = _out_proj(hf, a_out, b_out, c_outs, c_lses, w_out)
    hf = _expert_choice_ffn(hf, norm2_g, w_router, w_gate, w_up, w_down)
    return _ple(hf, p_i.reshape(N, -1), w_ple, ple_norm_g, gate_norm_g, w_ple_gate).reshape(B, T, D_MODEL)


def _trunk(h, p, lb, weights):
    (norm1_g, w_in, a_norm_g, b_qnorm_g, b_knorm_g, b_rpb, c_qnorm_g, c_knorm_g, w_out,
     norm2_g, w_router, w_gate, w_up, w_down, w_ple, ple_norm_g, gate_norm_g, w_ple_gate) = weights
    for i in range(DEPTH):
        h = _layer(h, p[i], lb[0, i], lb[1, i], norm1_g[i], w_in[i], a_norm_g[i], b_qnorm_g[i],
                   b_knorm_g[i], b_rpb[i], c_qnorm_g[i], c_knorm_g[i], w_out[i], norm2_g[i],
                   w_router[i], w_gate[i], w_up[i], w_down[i], w_ple[i], ple_norm_g[i],
                   gate_norm_g[i], w_ple_gate[i])
    return h


def kernel(x_prompt, x_sample, p_prompt, p_sample, norm1_g, w_in, lb_logits, a_norm_g,
           b_qnorm_g, b_knorm_g, b_rpb, c_qnorm_g, c_knorm_g, w_out, norm2_g, w_router,
           w_gate, w_up, w_down, w_ple, ple_norm_g, gate_norm_g, w_ple_gate):
    pr = jax.nn.softmax(lb_logits.astype(jnp.float32), axis=1)
    lb = jnp.cumsum(pr, axis=1) - pr[:, :1]
    weights = (norm1_g, w_in, a_norm_g, b_qnorm_g, b_knorm_g, b_rpb, c_qnorm_g, c_knorm_g, w_out,
               norm2_g, w_router, w_gate, w_up, w_down, w_ple, ple_norm_g, gate_norm_g, w_ple_gate)
    y_prompt = _trunk(x_prompt, p_prompt, lb, weights)
    y_sample = _trunk(x_sample, p_sample, lb, weights)
    return (y_prompt, y_sample)
```

```python
import functools

import jax
import jax.numpy as jnp
import numpy as np
from jax import lax
from jax.experimental import pallas as pl
from jax.experimental.pallas import tpu as pltpu

D_MODEL = 2048
DEPTH = 4
HEAD_DIM = 128
A_HEADS = 6
B_HEADS = 4
C_HEADS = 6
A_WIDTH = A_HEADS * HEAD_DIM
B_WIDTH = B_HEADS * HEAD_DIM
C_WIDTH = C_HEADS * HEAD_DIM
IN_WIDTH = 5 * A_WIDTH + 3 * B_WIDTH + 3 * C_WIDTH
HGRN_CHUNK = 128
GRID_W = 64
NA_WIN_R = 8
NA_WIN_C = 16
DIL_GROUPS = ((128, 1), (512, 4), (2048, 16))
C_HEADS_PER_GROUP = C_HEADS // len(DIL_GROUPS)
ROPE_THETA = 10000.0
N_EXPERTS = 16
EC_CAPACITY_FACTOR = 2
EPS = 1e-6

VMEM_LIMIT_BYTES = 56 * 1024 * 1024
NEG_INF = -1e30
Z_COLS = IN_WIDTH // HEAD_DIM
B_COL0 = 5 * A_HEADS
C_COL0 = B_COL0 + 3 * B_HEADS


def _sigmoid(x):
    return 1.0 / (1.0 + jnp.exp(-x))


def _bf16(x):
    return x.astype(jnp.bfloat16)


def _dot(a, b):
    return jnp.dot(_bf16(a), _bf16(b), preferred_element_type=jnp.float32)


def _dot_nt(a, b):
    return lax.dot_general(_bf16(a), _bf16(b), (((1,), (1,)), ((), ())),
                           preferred_element_type=jnp.float32)


def _dot_tn(a, b):
    return lax.dot_general(_bf16(a), _bf16(b), (((0,), (0,)), ((), ())),
                           preferred_element_type=jnp.float32)


def _rms(x, g):
    return x * lax.rsqrt(jnp.mean(x * x, axis=-1, keepdims=True) + EPS) * g


def _norm_matmul_kernel(x_ref, g_ref, w_ref, o_ref, xn_ref, *, normalize):
    @pl.when(pl.program_id(1) == 0)
    def _():
        x = x_ref[...]
        if normalize:
            x = _rms(x, g_ref[...])
        xn_ref[...] = x.astype(jnp.bfloat16)

    o_ref[...] = jnp.dot(xn_ref[...], w_ref[...], preferred_element_type=jnp.float32)


def _matmul(x, w, gain=None, *, tm=1024, tn=512):
    n, k = x.shape
    m = w.shape[1]
    tm = min(tm, n)
    tn = min(tn, m)
    assert n % tm == 0 and m % tn == 0
    normalize = gain is not None
    g = (gain if normalize else jnp.ones((k,), jnp.float32)).reshape(1, k).astype(jnp.float32)
    return pl.pallas_call(
        functools.partial(_norm_matmul_kernel, normalize=normalize),
        out_shape=jax.ShapeDtypeStruct((n, m), jnp.float32),
        grid=(n // tm, m // tn),
        in_specs=[
            pl.BlockSpec((tm, k), lambda i, j: (i, 0)),
            pl.BlockSpec((1, k), lambda i, j: (0, 0)),
            pl.BlockSpec((k, tn), lambda i, j: (0, j)),
        ],
        out_specs=pl.BlockSpec((tm, tn), lambda i, j: (i, j)),
        scratch_shapes=[pltpu.VMEM((tm, k), jnp.bfloat16)],
        compiler_params=pltpu.CompilerParams(
            dimension_semantics=("arbitrary", "arbitrary"),
            vmem_limit_bytes=VMEM_LIMIT_BYTES,
        ),
        name="norm_matmul" if normalize else "matmul",
    )(x, g, w.astype(jnp.bfloat16))


HGRN_LEVELS = tuple(HGRN_CHUNK >> (i + 1) for i in range(HGRN_CHUNK.bit_length() - 1))
HGRN_NLEV = len(HGRN_LEVELS)
HGRN_HPS = 2
HGRN_COLS = HGRN_HPS * HEAD_DIM


def _hgrn_tables():
    C = HGRN_CHUNK
    t = np.arange(C)[:, None]
    u = np.arange(C)[None, :]
    dmats, amasks = [], []
    for reverse in (False, True):
        cum = ((u >= t) if reverse else (u <= t)).astype(np.float32)
        blocks, masks = [], []
        for m in HGRN_LEVELS:
            pos = np.arange(C) % (2 * m)
            ref = np.arange(C) - pos + (m if reverse else m - 1)
            blocks.append(cum - cum[ref])
            later = (pos < m) if reverse else (pos >= m)
            same = (t // (2 * m)) == (u // (2 * m))
            masks.append((same & later[:, None] & ~later[None, :]).astype(np.float32))
        last = 0 if reverse else C - 1
        blocks += [cum, cum[last:last + 1] - cum]
        masks.append(np.eye(C, dtype=np.float32))
        assert np.array_equal(sum(masks), cum)
        dmats.append(np.tile(np.concatenate(blocks, axis=0), (1, 2)))
        amasks.append(np.stack(masks, axis=0))
    return np.stack(dmats, axis=0), np.stack(amasks, axis=0)


def _hgrn_intra(q, x, v, lb, dmat, amask, reverse):
    C = HGRN_CHUNK
    f = lb + (1.0 - lb) * _sigmoid(x)
    lf = jnp.log(f)
    k = 1.0 - f
    hi = _bf16(lf)
    mid = _bf16(lf - hi.astype(jnp.float32))
    dall = jnp.dot(dmat[...], jnp.concatenate([hi, mid], axis=0),
                   preferred_element_type=jnp.float32)

    def stack(a):
        return jnp.concatenate([a[:, j * HEAD_DIM:(j + 1) * HEAD_DIM] for j in range(HGRN_HPS)], axis=0)

    def head_blocks(p):
        return [p[j * C:(j + 1) * C, j * C:(j + 1) * C] for j in range(HGRN_HPS)]

    qs, ks = stack(q), stack(k)
    row = lax.broadcasted_iota(jnp.int32, (HGRN_HPS * C, HEAD_DIM), 0)
    a = [amask[HGRN_NLEV] * p for p in head_blocks(_dot_nt(qs, ks))]
    for l, m in enumerate(HGRN_LEVELS):
        later = ((row & m) == 0) if reverse else ((row & m) != 0)
        xh = _bf16(jnp.where(later, qs, ks) * jnp.exp(-jnp.abs(stack(dall[l * C:(l + 1) * C]))))
        pairs = lax.dot_general(xh, xh, (((1,), (1,)), ((), ())), preferred_element_type=jnp.float32)
        a = [a_j + amask[l] * p for a_j, p in zip(a, head_blocks(pairs))]
    oi = jnp.concatenate([_dot(a[j], v[:, j * HEAD_DIM:(j + 1) * HEAD_DIM]) for j in range(HGRN_HPS)],
                         axis=1)
    b = dall[HGRN_NLEV * C:(HGRN_NLEV + 1) * C]
    b_rest = dall[(HGRN_NLEV + 1) * C:(HGRN_NLEV + 2) * C]
    b_last = b[0:1, :] if reverse else b[C - 1:C, :]
    return oi, q * jnp.exp(b), k * jnp.exp(b_rest), jnp.exp(b_last)


def _hgrn_kernel(q_ref, ff_ref, fb_ref, v_ref, g_ref, lb_ref, ng_ref, dmat_ref, amask_ref, o_ref,
                 ofwd_ref, st_ref, oi_ref, qt_ref, kt_ref, eb_ref, *, n_tiles, tile_rows):
    s = pl.program_id(2)
    n_chunks = tile_rows // HGRN_CHUNK
    head_cols = [slice(j * HEAD_DIM, (j + 1) * HEAD_DIM) for j in range(HGRN_HPS)]

    @pl.when((s == 0) | (s == n_tiles))
    def _():
        st_ref[...] = jnp.zeros_like(st_ref)

    def intra_pass(f_ref, direction, reverse):
        def body(c, carry):
            rows = pl.ds(pl.multiple_of(c * HGRN_CHUNK, HGRN_CHUNK), HGRN_CHUNK)
            qp = q_ref[rows, :]
            oi, qt, kt, eb = _hgrn_intra(qp * _sigmoid(qp), f_ref[rows, :], v_ref[rows, :],
                                         lb_ref[direction:direction + 1, :], dmat_ref, amask_ref, reverse)
            oi_ref[rows, :] = oi
            qt_ref[rows, :] = _bf16(qt)
            kt_ref[rows, :] = _bf16(kt)
            eb_ref[pl.ds(pl.multiple_of(c * 8, 8), 8), :] = jnp.broadcast_to(eb, (8, HGRN_COLS))
            return carry

        lax.fori_loop(0, n_chunks, body, 0)

    def state_step(c):
        rows = pl.ds(pl.multiple_of(c * HGRN_CHUNK, HGRN_CHUNK), HGRN_CHUNK)
        outs = []
        for j, cols in enumerate(head_cols):
            st = st_ref[j]
            outs.append(oi_ref[rows, cols] + _dot_nt(qt_ref[rows, cols], st))
            st_ref[j] = st * eb_ref[pl.ds(pl.multiple_of(c * 8, 8), 1), cols] + _dot_tn(v_ref[rows, cols],
                                                                                       kt_ref[rows, cols])
        return rows, outs

    @pl.when(s < n_tiles)
    def _():
        intra_pass(ff_ref, 0, False)
        base = s * tile_rows

        def body(c, carry):
            rows, outs = state_step(c)
            dst = pl.ds(pl.multiple_of(base + c * HGRN_CHUNK, HGRN_CHUNK), HGRN_CHUNK)
            for j, cols in enumerate(head_cols):
                ofwd_ref[dst, cols] = outs[j]
            return carry

        lax.fori_loop(0, n_chunks, body, 0, unroll=2)

    @pl.when(s >= n_tiles)
    def _():
        intra_pass(fb_ref, 1, True)
        base = (2 * n_tiles - 1 - s) * tile_rows

        def body(ci, carry):
            c = n_chunks - 1 - ci
            rows, outs = state_step(c)
            src = pl.ds(pl.multiple_of(base + c * HGRN_CHUNK, HGRN_CHUNK), HGRN_CHUNK)
            for j, cols in enumerate(head_cols):
                y = _rms(outs[j] + ofwd_ref[src, cols], ng_ref[...])
                g = g_ref[rows, cols]
                o_ref[rows, cols] = y * (g * _sigmoid(g))
            return carry

        lax.fori_loop(0, n_chunks, body, 0, unroll=2)


def _hgrn_tile_rows(T):
    return min(T, 2048)


def _hgrn2_mixer(z, lb_f, lb_b, norm_g):
    B, T, _ = z.shape
    tile_rows = _hgrn_tile_rows(T)
    n_tiles = T // tile_rows
    n_groups = A_HEADS // HGRN_HPS
    lb = jnp.stack([lb_f.reshape(n_groups, HGRN_COLS), lb_b.reshape(n_groups, HGRN_COLS)], axis=1)
    last = n_tiles - 1
    dmat, amask = _hgrn_tables()

    def both(s):
        return jnp.where(s < n_tiles, s, 2 * n_tiles - 1 - s)

    def fwd_only(s):
        return jnp.minimum(s, last)

    def bwd_only(s):
        return jnp.where(s < n_tiles, last, 2 * n_tiles - 1 - s)

    def zspec(part, tile_of):
        return pl.BlockSpec((None, tile_rows, HGRN_COLS),
                            lambda b, h, s: (b, tile_of(s), part * n_groups + h))

    return pl.pallas_call(
        functools.partial(_hgrn_kernel, n_tiles=n_tiles, tile_rows=tile_rows),
        out_shape=jax.ShapeDtypeStruct((B, T, A_WIDTH), jnp.float32),
        grid=(B, n_groups, 2 * n_tiles),
        in_specs=[
            zspec(0, both), zspec(1, fwd_only), zspec(2, bwd_only), zspec(3, both), zspec(4, bwd_only),
            pl.BlockSpec((None, 2, HGRN_COLS), lambda b, h, s: (h, 0, 0)),
            pl.BlockSpec((1, HEAD_DIM), lambda b, h, s: (0, 0)),
            pl.BlockSpec((None,) + dmat.shape[1:], lambda b, h, s: (s // n_tiles, 0, 0)),
            pl.BlockSpec((None,) + amask.shape[1:], lambda b, h, s: (s // n_tiles, 0, 0, 0)),
        ],
        out_specs=pl.BlockSpec((None, tile_rows, HGRN_COLS), lambda b, h, s: (b, bwd_only(s), h)),
        scratch_shapes=[pltpu.VMEM((T, HGRN_COLS), jnp.float32),
                        pltpu.VMEM((HGRN_HPS, HEAD_DIM, HEAD_DIM), jnp.float32),
                        pltpu.VMEM((tile_rows, HGRN_COLS), jnp.float32),
                        pltpu.VMEM((tile_rows, HGRN_COLS), jnp.bfloat16),
                        pltpu.VMEM((tile_rows, HGRN_COLS), jnp.bfloat16),
                        pltpu.VMEM((tile_rows // HGRN_CHUNK * 8, HGRN_COLS), jnp.float32)],
        compiler_params=pltpu.CompilerParams(
            dimension_semantics=("arbitrary", "arbitrary", "arbitrary"),
            vmem_limit_bytes=VMEM_LIMIT_BYTES,
        ),
        name="hgrn2",
    )(z, z, z, z, z, lb.astype(jnp.float32), norm_g.reshape(1, HEAD_DIM).astype(jnp.float32),
      jnp.asarray(dmat, jnp.bfloat16), jnp.asarray(amask, jnp.float32))


NA_KEYS = NA_WIN_R * GRID_W


def _na_bias_table(rpb):
    qc = np.arange(GRID_W)[:, None]
    kc = np.arange(GRID_W)[None, :]
    ws = np.clip(qc - NA_WIN_C // 2, 0, GRID_W - NA_WIN_C)
    col_ok = (kc >= ws) & (kc < ws + NA_WIN_C)
    dc = np.clip(kc - qc + NA_WIN_C - 1, 0, 2 * NA_WIN_C - 2)
    toe = jnp.where(jnp.asarray(col_ok)[None, None], rpb.astype(jnp.float32)[:, :, dc], NEG_INF)
    cases = []
    for c in range(NA_WIN_R):
        cases.append(jnp.concatenate([toe[:, c + j] for j in range(NA_WIN_R)], axis=-1))
    return jnp.stack(cases, axis=1)


def _na_kernel(q_ref, k_ref, v_ref, bias_ref, qg_ref, kg_ref, o_ref, *, rows, tile_grid_rows):
    qt = pl.program_id(2)

    def body(rl, carry):
        r = qt * tile_grid_rows + rl
        rs = jnp.clip(r - NA_WIN_R // 2, 0, rows - NA_WIN_R)
        q0 = pl.multiple_of(rl * GRID_W, GRID_W)
        k0 = pl.multiple_of(rs * GRID_W, GRID_W)
        q = _rms(q_ref[pl.ds(q0, GRID_W), :], qg_ref[...]) * (1.0 / np.sqrt(HEAD_DIM))
        k = _rms(k_ref[pl.ds(k0, NA_KEYS), :], kg_ref[...])
        s = _dot_nt(q, k) + bias_ref[rs - r + NA_WIN_R - 1]
        m = jnp.max(s, axis=-1, keepdims=True)
        p = jnp.exp(s - m)
        den = jnp.sum(p, axis=-1, keepdims=True)
        o_ref[pl.ds(q0, GRID_W), :] = _dot(p, v_ref[pl.ds(k0, NA_KEYS), :]) / den
        return carry

    lax.fori_loop(0, tile_grid_rows, body, 0, unroll=2)


def _neighborhood_mixer(z, rpb, qnorm_g, knorm_g):
    B, T, _ = z.shape
    rows = T // GRID_W
    assert rows >= NA_WIN_R
    tile_grid_rows = min(rows, 32)
    tq = tile_grid_rows * GRID_W
    bias = _na_bias_table(rpb)
    gq = qnorm_g.reshape(1, HEAD_DIM).astype(jnp.float32)
    gk = knorm_g.reshape(1, HEAD_DIM).astype(jnp.float32)
    return pl.pallas_call(
        functools.partial(_na_kernel, rows=rows, tile_grid_rows=tile_grid_rows),
        out_shape=jax.ShapeDtypeStruct((B, T, B_WIDTH), jnp.float32),
        grid=(B, B_HEADS, T // tq),
        in_specs=[
            pl.BlockSpec((None, tq, HEAD_DIM), lambda b, h, t: (b, t, B_COL0 + h)),
            pl.BlockSpec((None, T, HEAD_DIM), lambda b, h, t: (b, 0, B_COL0 + B_HEADS + h)),
            pl.BlockSpec((None, T, HEAD_DIM), lambda b, h, t: (b, 0, B_COL0 + 2 * B_HEADS + h)),
            pl.BlockSpec((None, NA_WIN_R, GRID_W, NA_KEYS), lambda b, h, t: (h, 0, 0, 0)),
            pl.BlockSpec((1, HEAD_DIM), lambda b, h, t: (0, 0)),
            pl.BlockSpec((1, HEAD_DIM), lambda b, h, t: (0, 0)),
        ],
        out_specs=pl.BlockSpec((None, tq, HEAD_DIM), lambda b, h, t: (b, t, h)),
        compiler_params=pltpu.CompilerParams(
            dimension_semantics=("arbitrary", "arbitrary", "arbitrary"),
            vmem_limit_bytes=VMEM_LIMIT_BYTES,
        ),
        name="neighborhood_attn",
    )(z, z, z, bias, gq, gk)


DIL_HALF = 64
DIL_QB = 128
assert all(w // (2 * d) == DIL_HALF for w, d in DIL_GROUPS)


def _rope_tables(T):
    half = HEAD_DIM // 2
    inv = 1.0 / (ROPE_THETA ** (jnp.arange(half, dtype=jnp.float32) * 2.0 / HEAD_DIM))
    ang = jnp.arange(T, dtype=jnp.float32)[:, None] * inv[None, :]
    cos, sin = jnp.cos(ang), jnp.sin(ang)
    return jnp.concatenate([cos, cos], axis=-1), jnp.concatenate([-sin, sin], axis=-1)


def _dil_kernel(q_ref, k_ref, v_ref, qcos_ref, qsin_ref, kcos_ref, ksin_ref, qg_ref, kg_ref,
                o_ref, lse_ref, kn_ref, *, L, Lq, KW, QB, KB):
    qi = pl.program_id(3)

    def norm_rope(x, g, cos, sin):
        xn = _rms(x, g)
        return xn * cos + pltpu.roll(xn, HEAD_DIM // 2, 1) * sin

    @pl.when(qi == 0)
    def _():
        def kbody(i, carry):
            r = pl.ds(pl.multiple_of(i * KB, KB), KB)
            kn_ref[r, :] = _bf16(norm_rope(k_ref[r, :], kg_ref[...], kcos_ref[r, :], ksin_ref[r, :]))
            return carry

        lax.fori_loop(0, L // KB, kbody, 0)

    def body(i, carry):
        q0l = pl.multiple_of(i * QB, QB)
        q0 = qi * Lq + q0l
        kb = pl.multiple_of(jnp.clip(q0 - DIL_HALF, 0, L - KW), DIL_HALF)
        rows = pl.ds(q0l, QB)
        q = norm_rope(q_ref[rows, :], qg_ref[...], qcos_ref[rows, :], qsin_ref[rows, :])
        s = _dot_nt(q * (1.0 / np.sqrt(HEAD_DIM)), kn_ref[pl.ds(kb, KW), :])
        qpos = q0 + lax.broadcasted_iota(jnp.int32, (QB, KW), 0)
        kpos = kb + lax.broadcasted_iota(jnp.int32, (QB, KW), 1)
        s = jnp.where(jnp.abs(qpos - kpos) <= DIL_HALF, s, NEG_INF)
        m = jnp.max(s, axis=-1, keepdims=True)
        p = jnp.exp(s - m)
        den = jnp.sum(p, axis=-1, keepdims=True)
        o_ref[rows, :] = _dot(p, v_ref[pl.ds(kb, KW), :]) / den
        lse_ref[rows, :] = jnp.broadcast_to(m + jnp.log(den), (QB, HEAD_DIM))
        return carry

    lax.fori_loop(0, Lq // QB, body, 0, unroll=2 if Lq // QB >= 2 else 1)


C_GROUP_COLS = 3 * C_HEADS_PER_GROUP * HEAD_DIM


def _c_proj_kernel(x_ref, g_ref, w_ref, o0_ref, o1_ref, o2_ref, xn_ref, res_ref, *, tm):
    gi = pl.program_id(1)

    @pl.when(gi == 0)
    def _():
        xn_ref[...] = _bf16(_rms(x_ref[...], g_ref[...]))

    res = jnp.dot(xn_ref[...], w_ref[...], preferred_element_type=jnp.float32)
    for g, ((_, d), o_ref) in enumerate(zip(DIL_GROUPS, (o0_ref, o1_ref, o2_ref))):
        @pl.when(gi == g)
        def _(d=d, o_ref=o_ref):
            if d == 1:
                o_ref[...] = res
            else:
                for c in range(C_GROUP_COLS // LANES):
                    res_ref[c] = res[:, c * LANES:(c + 1) * LANES]
                for j in range(d):
                    for c in range(C_GROUP_COLS // LANES):
                        c0 = j * C_GROUP_COLS + c * LANES
                        o_ref[:, c0:c0 + LANES] = res_ref[c, pl.ds(j, tm // d, stride=d), :]


def _c_proj(hf, gain, w_c, *, tm=512):
    n, k = hf.shape
    hp = C_HEADS_PER_GROUP
    cols = []
    for g in range(len(DIL_GROUPS)):
        for part in range(3):
            c0 = part * C_WIDTH + g * hp * HEAD_DIM
            cols.append(w_c[:, c0:c0 + hp * HEAD_DIM])
    w_perm = jnp.concatenate(cols, axis=1).astype(jnp.bfloat16)
    return pl.pallas_call(
        functools.partial(_c_proj_kernel, tm=tm),
        out_shape=tuple(jax.ShapeDtypeStruct((n // d, d * C_GROUP_COLS), jnp.float32) for _, d in DIL_GROUPS),
        grid=(n // tm, len(DIL_GROUPS)),
        in_specs=[pl.BlockSpec((tm, k), lambda i, g: (i, 0)),
                  pl.BlockSpec((1, k), lambda i, g: (0, 0)),
                  pl.BlockSpec((k, C_GROUP_COLS), lambda i, g: (0, g))],
        out_specs=tuple(pl.BlockSpec((tm // d, d * C_GROUP_COLS), lambda i, g: (i, 0)) for _, d in DIL_GROUPS),
        scratch_shapes=[pltpu.VMEM((tm, k), jnp.bfloat16),
                        pltpu.VMEM((C_GROUP_COLS // LANES, tm, LANES), jnp.float32)],
        compiler_params=pltpu.CompilerParams(dimension_semantics=("arbitrary", "arbitrary"),
                                             vmem_limit_bytes=VMEM_LIMIT_BYTES),
        name="c_proj",
    )(hf, gain.reshape(1, k).astype(jnp.float32), w_perm)


def _dilated_group(zg, B, T, d, qnorm_g, knorm_g, cos2, sin2):
    hp = C_HEADS_PER_GROUP
    L = T // d
    Lq = min(L, 1024)
    QB = min(DIL_QB, Lq)
    KW = min(2 * DIL_QB, L)
    KB = min(L, 512)
    zv = zg.reshape(B, L, d * C_GROUP_COLS)
    cosv = cos2.reshape(L, d * HEAD_DIM)
    sinv = sin2.reshape(L, d * HEAD_DIM)
    gq = qnorm_g.reshape(1, HEAD_DIM).astype(jnp.float32)
    gk = knorm_g.reshape(1, HEAD_DIM).astype(jnp.float32)

    def zcol(part):
        return lambda b, j, h, qi: (j * 3 + part) * hp + h

    out_sds = jax.ShapeDtypeStruct((B, L, d * hp * HEAD_DIM), jnp.float32)
    out_spec = pl.BlockSpec((None, Lq, HEAD_DIM), lambda b, j, h, qi: (b, qi, j * hp + h))
    o, lse = pl.pallas_call(
        functools.partial(_dil_kernel, L=L, Lq=Lq, KW=KW, QB=QB, KB=KB),
        out_shape=(out_sds, out_sds),
        grid=(B, d, hp, L // Lq),
        in_specs=[
            pl.BlockSpec((None, Lq, HEAD_DIM), lambda b, j, h, qi: (b, qi, zcol(0)(b, j, h, qi))),
            pl.BlockSpec((None, L, HEAD_DIM), lambda b, j, h, qi: (b, 0, zcol(1)(b, j, h, qi))),
            pl.BlockSpec((None, L, HEAD_DIM), lambda b, j, h, qi: (b, 0, zcol(2)(b, j, h, qi))),
            pl.BlockSpec((Lq, HEAD_DIM), lambda b, j, h, qi: (qi, j)),
            pl.BlockSpec((Lq, HEAD_DIM), lambda b, j, h, qi: (qi, j)),
            pl.BlockSpec((L, HEAD_DIM), lambda b, j, h, qi: (0, j)),
            pl.BlockSpec((L, HEAD_DIM), lambda b, j, h, qi: (0, j)),
            pl.BlockSpec((1, HEAD_DIM), lambda b, j, h, qi: (0, 0)),
            pl.BlockSpec((1, HEAD_DIM), lambda b, j, h, qi: (0, 0)),
        ],
        out_specs=(out_spec, out_spec),
        scratch_shapes=[pltpu.VMEM((L, HEAD_DIM), jnp.bfloat16)],
        compiler_params=pltpu.CompilerParams(
            dimension_semantics=("arbitrary", "arbitrary", "arbitrary", "arbitrary"),
            vmem_limit_bytes=VMEM_LIMIT_BYTES,
        ),
        name=f"dilated_attn_d{d}",
    )(zv, zv, zv, cosv, sinv, cosv, sinv, gq, gk)
    return o.reshape(B * L, d * hp * HEAD_DIM), lse.reshape(B * L, d * hp * HEAD_DIM)


def _dilated_mixer(zgs, B, T, qnorm_g, knorm_g):
    cos2, sin2 = _rope_tables(T)
    outs, lses = [], []
    for zg, (_, d) in zip(zgs, DIL_GROUPS):
        o, lse = _dilated_group(zg, B, T, d, qnorm_g, knorm_g, cos2, sin2)
        outs.append(o)
        lses.append(lse)
    return outs, lses


def _out_proj_kernel(h_ref, a_ref, b_ref, o0_ref, o1_ref, o2_ref, l0_ref, l1_ref, l2_ref, w_ref,
                     out_ref, mix_ref, stage_ref, *, tm):
    gw = C_HEADS_PER_GROUP * HEAD_DIM

    def token_major(src_ref, d, slot):
        if d == 1:
            return src_ref[...]
        nc = gw // LANES
        for j in range(d):
            for c in range(nc):
                c0 = j * gw + c * LANES
                stage_ref[slot * nc + c, pl.ds(j, tm // d, stride=d), :] = src_ref[:, c0:c0 + LANES]
        return jnp.concatenate([stage_ref[slot * nc + c] for c in range(nc)], axis=1)

    @pl.when(pl.program_id(1) == 0)
    def _():
        mix_ref[:, :A_WIDTH] = _bf16(a_ref[...])
        mix_ref[:, A_WIDTH:A_WIDTH + B_WIDTH] = _bf16(b_ref[...])
        dils = [d for _, d in DIL_GROUPS]
        lses = [token_major(l_ref, d, g) for g, (l_ref, d) in enumerate(zip((l0_ref, l1_ref, l2_ref), dils))]
        m = jnp.maximum(jnp.maximum(lses[0], lses[1]), lses[2])
        es = [jnp.exp(l - m) for l in lses]
        inv = 1.0 / (es[0] + es[1] + es[2])
        for g, (o_ref, d) in enumerate(zip((o0_ref, o1_ref, o2_ref), dils)):
            c0 = A_WIDTH + B_WIDTH + g * gw
            mix_ref[:, c0:c0 + gw] = _bf16(token_major(o_ref, d, len(dils) + g) * (es[g] * inv))

    out_ref[...] = h_ref[...] + jnp.dot(mix_ref[...], w_ref[...], preferred_element_type=jnp.float32)


def _out_proj(hf, a_out, b_out, c_outs, c_lses, w_out, *, tm=512, tn=512):
    n = hf.shape[0]
    gw = C_HEADS_PER_GROUP * HEAD_DIM
    row = lambda width: pl.BlockSpec((tm, width), lambda i, j: (i, 0))
    dil = [pl.BlockSpec((tm // d, d * gw), lambda i, j: (i, 0)) for _, d in DIL_GROUPS]
    return pl.pallas_call(
        functools.partial(_out_proj_kernel, tm=tm),
        out_shape=jax.ShapeDtypeStruct((n, D_MODEL), jnp.float32),
        grid=(n // tm, D_MODEL // tn),
        in_specs=[pl.BlockSpec((tm, tn), lambda i, j: (i, j)), row(A_WIDTH), row(B_WIDTH)]
        + dil + dil + [pl.BlockSpec((D_MODEL, tn), lambda i, j: (0, j))],
        out_specs=pl.BlockSpec((tm, tn), lambda i, j: (i, j)),
        scratch_shapes=[pltpu.VMEM((tm, D_MODEL), jnp.bfloat16),
                        pltpu.VMEM((2 * len(DIL_GROUPS) * gw // LANES, tm, LANES), jnp.float32)],
        compiler_params=pltpu.CompilerParams(
            dimension_semantics=("arbitrary", "arbitrary"),
            vmem_limit_bytes=VMEM_LIMIT_BYTES,
        ),
        name="out_proj",
    )(hf, a_out.reshape(n, A_WIDTH), b_out.reshape(n, B_WIDTH), *c_outs, *c_lses,
      w_out.astype(jnp.bfloat16))


def _ple_kernel(hrow_ref, h_ref, p_ref, gg_ref, pg_ref, wp_ref, wg_ref, out_ref, xn_ref, e_ref, *, tn):
    j = pl.program_id(1)

    @pl.when(j == 0)
    def _():
        xn_ref[...] = _bf16(_rms(hrow_ref[...], gg_ref[...]))
        e = _rms(jnp.dot(_bf16(p_ref[...]), wp_ref[...], preferred_element_type=jnp.float32), pg_ref[...])
        for jj in range(D_MODEL // tn):
            e_ref[jj] = e[:, jj * tn:(jj + 1) * tn]

    gate = _sigmoid(jnp.dot(xn_ref[...], wg_ref[...], preferred_element_type=jnp.float32))
    out_ref[...] = h_ref[...] + gate * e_ref[j]


def _ple(hf, p, w_ple, ple_norm_g, gate_norm_g, w_ple_gate, *, tm=512, tn=512):
    n = hf.shape[0]
    pd = p.shape[1]
    vec = lambda g: g.reshape(1, D_MODEL).astype(jnp.float32)
    return pl.pallas_call(
        functools.partial(_ple_kernel, tn=tn),
        out_shape=jax.ShapeDtypeStruct((n, D_MODEL), jnp.float32),
        grid=(n // tm, D_MODEL // tn),
        in_specs=[
            pl.BlockSpec((tm, D_MODEL), lambda i, j: (i, 0)),
            pl.BlockSpec((tm, tn), lambda i, j: (i, j)),
            pl.BlockSpec((tm, pd), lambda i, j: (i, 0)),
            pl.BlockSpec((1, D_MODEL), lambda i, j: (0, 0)),
            pl.BlockSpec((1, D_MODEL), lambda i, j: (0, 0)),
            pl.BlockSpec((pd, D_MODEL), lambda i, j: (0, 0)),
            pl.BlockSpec((D_MODEL, tn), lambda i, j: (0, j)),
        ],
        out_specs=pl.BlockSpec((tm, tn), lambda i, j: (i, j)),
        scratch_shapes=[pltpu.VMEM((tm, D_MODEL), jnp.bfloat16),
                        pltpu.VMEM((D_MODEL // tn, tm, tn), jnp.float32)],
        compiler_params=pltpu.CompilerParams(
            dimension_semantics=("arbitrary", "arbitrary"),
            vmem_limit_bytes=VMEM_LIMIT_BYTES,
        ),
        name="ple",
    )(hf, hf, p, vec(gate_norm_g), vec(ple_norm_g), w_ple.astype(jnp.bfloat16),
      w_ple_gate.astype(jnp.bfloat16))


LANES = 128
FFN_ROWS = 1024
HALF_D = D_MODEL // 2
COMBINE_ROWS = 256


def _pack_bf16_pairs(lo, hi):
    bits = lambda v: lax.bitcast_convert_type(_bf16(v).astype(jnp.float32), jnp.uint32)
    return (bits(lo) >> 16) | (bits(hi) & jnp.uint32(0xFFFF0000))


def _unpack_bf16_pairs(words):
    return (lax.bitcast_convert_type(words << 16, jnp.float32),
            lax.bitcast_convert_type(words & jnp.uint32(0xFFFF0000), jnp.float32))


def _router_kernel(h_ref, g_ref, wr_ref, xn_ref, aff_ref):
    xn = _rms(h_ref[...], g_ref[...])
    xn_ref[...] = _bf16(xn)
    logits = jnp.dot(xn, wr_ref[...], preferred_element_type=jnp.float32, precision=lax.Precision.HIGHEST)
    lane = lax.broadcasted_iota(jnp.int32, logits.shape, 1)
    logits = jnp.where(lane < N_EXPERTS, logits, NEG_INF)
    e = jnp.exp(logits - jnp.max(logits, axis=-1, keepdims=True))
    aff_ref[...] = e / jnp.sum(e, axis=-1, keepdims=True)


def _router(hf, norm_g, w_router, *, tm=512):
    n = hf.shape[0]
    wr = jnp.zeros((D_MODEL, LANES), jnp.float32).at[:, :N_EXPERTS].set(w_router.astype(jnp.float32))
    return pl.pallas_call(
        _router_kernel,
        out_shape=(jax.ShapeDtypeStruct((n, D_MODEL), jnp.bfloat16),
                   jax.ShapeDtypeStruct((n, LANES), jnp.float32)),
        grid=(n // tm,),
        in_specs=[pl.BlockSpec((tm, D_MODEL), lambda i: (i, 0)),
                  pl.BlockSpec((1, D_MODEL), lambda i: (0, 0)),
                  pl.BlockSpec((D_MODEL, LANES), lambda i: (0, 0))],
        out_specs=(pl.BlockSpec((tm, D_MODEL), lambda i: (i, 0)),
                   pl.BlockSpec((tm, LANES), lambda i: (i, 0))),
        compiler_params=pltpu.CompilerParams(dimension_semantics=("arbitrary",),
                                             vmem_limit_bytes=VMEM_LIMIT_BYTES),
        name="router",
    )(hf, norm_g.reshape(1, D_MODEL).astype(jnp.float32), wr)


def _expert_kernel(dest_hbm, x_ref, g_ref, wg_ref, wu_ref, wd_ref, y_hbm, ybuf, dbuf, ysem, dsem,
                   *, n_steps, steps_per_expert):
    step = pl.program_id(0) * steps_per_expert + pl.program_id(1)
    slot = step % 2

    def dest_copy(s, sl):
        return pltpu.make_async_copy(dest_hbm.at[s], dbuf.at[sl], dsem.at[sl])

    def scatter_wait(sl):
        pltpu.make_async_copy(ybuf.at[sl], ybuf.at[sl], ysem.at[sl]).wait()

    @pl.when(step == 0)
    def _():
        dest_copy(0, 0).start()

    @pl.when(step + 1 < n_steps)
    def _():
        dest_copy(step + 1, 1 - slot).start()

    @pl.when(step >= 2)
    def _():
        scatter_wait(slot)

    x = x_ref[...]
    gate = jnp.dot(x, wg_ref[...], preferred_element_type=jnp.float32)
    hdn = gate * _sigmoid(gate) * jnp.dot(x, wu_ref[...], preferred_element_type=jnp.float32)
    y = jnp.dot(_bf16(hdn), wd_ref[...], preferred_element_type=jnp.float32)
    y = y * jnp.concatenate([g_ref[...]] * (D_MODEL // LANES), axis=1)
    ybuf[slot] = _pack_bf16_pairs(y[:, :HALF_D], y[:, HALF_D:])

    dest_copy(step, slot).wait()
    for r in range(FFN_ROWS):
        pltpu.make_async_copy(ybuf.at[slot, pl.ds(r, 1), :],
                              y_hbm.at[pl.ds(dbuf[slot, r], 1), :], ysem.at[slot]).start()

    @pl.when(step == n_steps - 1)
    def _():
        scatter_wait(slot)
        if n_steps >= 2:
            scatter_wait(1 - slot)


def _expert_ffn(xe, gates_rep, dest, w_gate, w_up, w_down, n_tokens):
    rows, d = xe.shape
    cap = rows // N_EXPERTS
    f = w_gate.shape[-1]
    spe = cap // FFN_ROWS
    n_steps = N_EXPERTS * spe
    return pl.pallas_call(
        functools.partial(_expert_kernel, n_steps=n_steps, steps_per_expert=spe),
        out_shape=jax.ShapeDtypeStruct((N_EXPERTS * n_tokens, HALF_D), jnp.uint32),
        grid=(N_EXPERTS, spe),
        in_specs=[
            pl.BlockSpec(memory_space=pl.ANY),
            pl.BlockSpec((FFN_ROWS, d), lambda e, t: (e * spe + t, 0)),
            pl.BlockSpec((FFN_ROWS, LANES), lambda e, t: (e * spe + t, 0)),
            pl.BlockSpec((None, d, f), lambda e, t: (e, 0, 0), pipeline_mode=pl.Buffered(1)),
            pl.BlockSpec((None, d, f), lambda e, t: (e, 0, 0), pipeline_mode=pl.Buffered(1)),
            pl.BlockSpec((None, f, d), lambda e, t: (e, 0, 0), pipeline_mode=pl.Buffered(1)),
        ],
        out_specs=pl.BlockSpec(memory_space=pl.ANY),
        scratch_shapes=[pltpu.VMEM((2, FFN_ROWS, HALF_D), jnp.uint32),
                        pltpu.SMEM((2, FFN_ROWS), jnp.int32),
                        pltpu.SemaphoreType.DMA((2,)),
                        pltpu.SemaphoreType.DMA((2,))],
        compiler_params=pltpu.CompilerParams(dimension_semantics=("arbitrary", "arbitrary"),
                                             vmem_limit_bytes=VMEM_LIMIT_BYTES),
        name="expert_ffn",
    )(dest.reshape(n_steps, FFN_ROWS), xe, gates_rep, w_gate.astype(jnp.bfloat16),
      w_up.astype(jnp.bfloat16), w_down.astype(jnp.bfloat16))


def _combine_kernel(tmax_ref, h_ref, cnt_ref, y_ref, o_ref):
    i, r = pl.program_id(0), pl.program_id(1)

    @pl.when(r == 0)
    def _():
        o_ref[...] = h_ref[...]

    @pl.when(r < tmax_ref[i])
    def _():
        chosen = jnp.concatenate([cnt_ref[...]] * (HALF_D // LANES), axis=1) > r
        lo, hi = _unpack_bf16_pairs(y_ref[...])
        o_ref[:, :HALF_D] += jnp.where(chosen, lo, 0.0)
        o_ref[:, HALF_D:] += jnp.where(chosen, hi, 0.0)


def _combine(hf, slabs, cnt_rep, tile_max):
    n, d = hf.shape
    tm = COMBINE_ROWS

    def slab_index(i, r, tmax):
        return (jnp.minimum(r, jnp.maximum(tmax[i], 1) - 1), i, 0)

    return pl.pallas_call(
        _combine_kernel,
        out_shape=jax.ShapeDtypeStruct((n, d), jnp.float32),
        grid_spec=pltpu.PrefetchScalarGridSpec(
            num_scalar_prefetch=1,
            grid=(n // tm, N_EXPERTS),
            in_specs=[pl.BlockSpec((tm, d), lambda i, r, tmax: (i, 0)),
                      pl.BlockSpec((tm, LANES), lambda i, r, tmax: (i, 0)),
                      pl.BlockSpec((None, tm, HALF_D), slab_index)],
            out_specs=pl.BlockSpec((tm, d), lambda i, r, tmax: (i, 0)),
        ),
        compiler_params=pltpu.CompilerParams(dimension_semantics=("arbitrary", "arbitrary"),
                                             vmem_limit_bytes=VMEM_LIMIT_BYTES),
        name="ffn_combine",
    )(tile_max, hf, cnt_rep, slabs.reshape(N_EXPERTS, n, HALF_D))


def _expert_choice_ffn(hf, norm_g, w_router, w_gate, w_up, w_down):
    n = hf.shape[0]
    cap = (EC_CAPACITY_FACTOR * n) // N_EXPERTS
    xn, aff = _router(hf, norm_g, w_router)
    gates, idx = lax.top_k(aff[:, :N_EXPERTS].T, cap)
    xe = xn[idx.reshape(-1)]
    chosen = jnp.zeros((N_EXPERTS, n), jnp.int32).at[jnp.arange(N_EXPERTS)[:, None], idx].set(1)
    rank = jnp.cumsum(chosen, axis=0) - chosen
    dest = jnp.take_along_axis(rank, idx, axis=1) * n + idx
    cnt = jnp.sum(chosen, axis=0)
    slabs = _expert_ffn(xe, jnp.broadcast_to(gates.reshape(-1, 1), (N_EXPERTS * cap, LANES)),
                        dest.reshape(-1).astype(jnp.int32), w_gate, w_up, w_down, n)
    cnt_rep = jnp.broadcast_to(cnt[:, None], (n, LANES)).astype(jnp.int32)
    tile_max = jnp.max(cnt.reshape(n // COMBINE_ROWS, COMBINE_ROWS), axis=1).astype(jnp.int32)
    return _combine(hf, slabs, cnt_rep, tile_max)


def _layer(h, p_i, lb_f, lb_b, norm1_g, w_in, a_norm_g, b_qnorm_g, b_knorm_g, b_rpb,
           c_qnorm_g, c_knorm_g, w_out, norm2_g, w_router, w_gate, w_up, w_down,
           w_ple, ple_norm_g, gate_norm_g, w_ple_gate):
    B, T, _ = h.shape
    N = B * T
    hf = h.reshape(N, D_MODEL)
    ab_width = IN_WIDTH - 3 * C_WIDTH
    z_ab = _matmul(hf, w_in[:, :ab_width], norm1_g, tn=768).reshape(B, T, ab_width)
    z_c = _c_proj(hf, norm1_g, w_in[:, ab_width:])
    a_out = _hgrn2_mixer(z_ab, lb_f, lb_b, a_norm_g)
    b_out = _neighborhood_mixer(z_ab, b_rpb, b_qnorm_g, b_knorm_g)
    c_outs, c_lses = _dilated_mixer(z_c, B, T, c_qnorm_g, c_knorm_g)
    hf = _out_proj(hf, a_out, b_out, c_outs, c_lses, w_out)
    hf = _expert_choice_ffn(hf, norm2_g, w_router, w_gate, w_up, w_down)
    return _ple(hf, p_i.reshape(N, -1), w_ple, ple_norm_g, gate_norm_g, w_ple_gate).reshape(B, T, D_MODEL)


def _trunk(h, p, lb, weights):
    (norm1_g, w_in, a_norm_g, b_qnorm_g, b_knorm_g, b_rpb, c_qnorm_g, c_knorm_g, w_out,
     norm2_g, w_router, w_gate, w_up, w_down, w_ple, ple_norm_g, gate_norm_g, w_ple_gate) = weights
    for i in range(DEPTH):
        h = _layer(h, p[i], lb[0, i], lb[1, i], norm1_g[i], w_in[i], a_norm_g[i], b_qnorm_g[i],
                   b_knorm_g[i], b_rpb[i], c_qnorm_g[i], c_knorm_g[i], w_out[i], norm2_g[i],
                   w_router[i], w_gate[i], w_up[i], w_down[i], w_ple[i], ple_norm_g[i],
                   gate_norm_g[i], w_ple_gate[i])
    return h


def kernel(x_prompt, x_sample, p_prompt, p_sample, norm1_g, w_in, lb_logits, a_norm_g,
           b_qnorm_g, b_knorm_g, b_rpb, c_qnorm_g, c_knorm_g, w_out, norm2_g, w_router,
           w_gate, w_up, w_down, w_ple, ple_norm_g, gate_norm_g, w_ple_gate):
    pr = jax.nn.softmax(lb_logits.astype(jnp.float32), axis=1)
    lb = jnp.cumsum(pr, axis=1) - pr[:, :1]
    weights = (norm1_g, w_in, a_norm_g, b_qnorm_g, b_knorm_g, b_rpb, c_qnorm_g, c_knorm_g, w_out,
               norm2_g, w_router, w_gate, w_up, w_down, w_ple, ple_norm_g, gate_norm_g, w_ple_gate)
    y_prompt = _trunk(x_prompt, p_prompt, lb, weights)
    y_sample = _trunk(x_sample, p_sample, lb, weights)
    return (y_prompt, y_sample)
```

```python
import functools

import jax
import jax.numpy as jnp
import numpy as np
from jax import lax
from jax.experimental import pallas as pl
from jax.experimental.pallas import tpu as pltpu

D_MODEL = 2048
DEPTH = 4
HEAD_DIM = 128
A_HEADS = 6
B_HEADS = 4
C_HEADS = 6
A_WIDTH = A_HEADS * HEAD_DIM
B_WIDTH = B_HEADS * HEAD_DIM
C_WIDTH = C_HEADS * HEAD_DIM
IN_WIDTH = 5 * A_WIDTH + 3 * B_WIDTH + 3 * C_WIDTH
HGRN_CHUNK = 128
GRID_W = 64
NA_WIN_R = 8
NA_WIN_C = 16
DIL_GROUPS = ((128, 1), (512, 4), (2048, 16))
C_HEADS_PER_GROUP = C_HEADS // len(DIL_GROUPS)
ROPE_THETA = 10000.0
N_EXPERTS = 16
EC_CAPACITY_FACTOR = 2
EPS = 1e-6

VMEM_LIMIT_BYTES = 56 * 1024 * 1024
NEG_INF = -1e30
Z_COLS = IN_WIDTH // HEAD_DIM
B_COL0 = 5 * A_HEADS
C_COL0 = B_COL0 + 3 * B_HEADS


def _sigmoid(x):
    return 1.0 / (1.0 + jnp.exp(-x))


def _bf16(x):
    return x.astype(jnp.bfloat16)


def _dot(a, b):
    return jnp.dot(_bf16(a), _bf16(b), preferred_element_type=jnp.float32)


def _dot_nt(a, b):
    return lax.dot_general(_bf16(a), _bf16(b), (((1,), (1,)), ((), ())),
                           preferred_element_type=jnp.float32)


def _dot_tn(a, b):
    return lax.dot_general(_bf16(a), _bf16(b), (((0,), (0,)), ((), ())),
                           preferred_element_type=jnp.float32)


def _rms(x, g):
    return x * lax.rsqrt(jnp.mean(x * x, axis=-1, keepdims=True) + EPS) * g


def _norm_matmul_kernel(x_ref, g_ref, w_ref, o_ref, xn_ref, *, normalize):
    @pl.when(pl.program_id(1) == 0)
    def _():
        x = x_ref[...]
        if normalize:
            x = _rms(x, g_ref[...])
        xn_ref[...] = x.astype(jnp.bfloat16)

    o_ref[...] = jnp.dot(xn_ref[...], w_ref[...], preferred_element_type=jnp.float32)


def _matmul(x, w, gain=None, *, tm=1024, tn=512):
    n, k = x.shape
    m = w.shape[1]
    tm = min(tm, n)
    tn = min(tn, m)
    assert n % tm == 0 and m % tn == 0
    normalize = gain is not None
    g = (gain if normalize else jnp.ones((k,), jnp.float32)).reshape(1, k).astype(jnp.float32)
    return pl.pallas_call(
        functools.partial(_norm_matmul_kernel, normalize=normalize),
        out_shape=jax.ShapeDtypeStruct((n, m), jnp.float32),
        grid=(n // tm, m // tn),
        in_specs=[
            pl.BlockSpec((tm, k), lambda i, j: (i, 0)),
            pl.BlockSpec((1, k), lambda i, j: (0, 0)),
            pl.BlockSpec((k, tn), lambda i, j: (0, j)),
        ],
        out_specs=pl.BlockSpec((tm, tn), lambda i, j: (i, j)),
        scratch_shapes=[pltpu.VMEM((tm, k), jnp.bfloat16)],
        compiler_params=pltpu.CompilerParams(
            dimension_semantics=("arbitrary", "arbitrary"),
            vmem_limit_bytes=VMEM_LIMIT_BYTES,
        ),
        name="norm_matmul" if normalize else "matmul",
    )(x, g, w.astype(jnp.bfloat16))


HGRN_LEVELS = tuple(HGRN_CHUNK >> (i + 1) for i in range(HGRN_CHUNK.bit_length() - 1))
HGRN_NLEV = len(HGRN_LEVELS)
HGRN_HPS = 2
HGRN_COLS = HGRN_HPS * HEAD_DIM


def _hgrn_tables():
    C = HGRN_CHUNK
    t = np.arange(C)[:, None]
    u = np.arange(C)[None, :]
    dmats, amasks = [], []
    for reverse in (False, True):
        cum = ((u >= t) if reverse else (u <= t)).astype(np.float32)
        blocks, masks = [], []
        for m in HGRN_LEVELS:
            pos = np.arange(C) % (2 * m)
            ref = np.arange(C) - pos + (m if reverse else m - 1)
            blocks.append(cum - cum[ref])
            later = (pos < m) if reverse else (pos >= m)
            same = (t // (2 * m)) == (u // (2 * m))
            masks.append((same & later[:, None] & ~later[None, :]).astype(np.float32))
        last = 0 if reverse else C - 1
        blocks += [cum, cum[last:last + 1] - cum]
        masks.append(np.eye(C, dtype=np.float32))
        assert np.array_equal(sum(masks), cum)
        dmats.append(np.tile(np.concatenate(blocks, axis=0), (1, 2)))
        amasks.append(np.stack(masks, axis=0))
    return np.stack(dmats, axis=0), np.stack(amasks, axis=0)


def _hgrn_intra(q, x, v, lb, dmat, amask, reverse):
    C = HGRN_CHUNK
    f = lb + (1.0 - lb) * _sigmoid(x)
    lf = jnp.log(f)
    k = 1.0 - f
    hi = _bf16(lf)
    mid = _bf16(lf - hi.astype(jnp.float32))
    dall = jnp.dot(dmat[...], jnp.concatenate([hi, mid], axis=0),
                   preferred_element_type=jnp.float32)

    def stack(a):
        return jnp.concatenate([a[:, j * HEAD_DIM:(j + 1) * HEAD_DIM] for j in range(HGRN_HPS)], axis=0)

    def head_blocks(p):
        return [p[j * C:(j + 1) * C, j * C:(j + 1) * C] for j in range(HGRN_HPS)]

    qs, ks = stack(q), stack(k)
    row = lax.broadcasted_iota(jnp.int32, (HGRN_HPS * C, HEAD_DIM), 0)
    a = [amask[HGRN_NLEV] * p for p in head_blocks(_dot_nt(qs, ks))]
    for l, m in enumerate(HGRN_LEVELS):
        later = ((row & m) == 0) if reverse else ((row & m) != 0)
        xh = _bf16(jnp.where(later, qs, ks) * jnp.exp(-jnp.abs(stack(dall[l * C:(l + 1) * C]))))
        pairs = lax.dot_general(xh, xh, (((1,), (1,)), ((), ())), preferred_element_type=jnp.float32)
        a = [a_j + amask[l] * p for a_j, p in zip(a, head_blocks(pairs))]
    oi = jnp.concatenate([_dot(a[j], v[:, j * HEAD_DIM:(j + 1) * HEAD_DIM]) for j in range(HGRN_HPS)],
                         axis=1)
    b = dall[HGRN_NLEV * C:(HGRN_NLEV + 1) * C]
    b_rest = dall[(HGRN_NLEV + 1) * C:(HGRN_NLEV + 2) * C]
    b_last = b[0:1, :] if reverse else b[C - 1:C, :]
    return oi, q * jnp.exp(b), k * jnp.exp(b_rest), jnp.exp(b_last)


def _hgrn_kernel(q_ref, ff_ref, fb_ref, v_ref, g_ref, lb_ref, ng_ref, dmat_ref, amask_ref, o_ref,
                 ofwd_ref, st_ref, oi_ref, qt_ref, kt_ref, eb_ref, *, n_tiles, tile_rows):
    s = pl.program_id(2)
    n_chunks = tile_rows // HGRN_CHUNK
    head_cols = [slice(j * HEAD_DIM, (j + 1) * HEAD_DIM) for j in range(HGRN_HPS)]

    @pl.when((s == 0) | (s == n_tiles))
    def _():
        st_ref[...] = jnp.zeros_like(st_ref)

    def intra_pass(f_ref, direction, reverse):
        def body(c, carry):
            rows = pl.ds(pl.multiple_of(c * HGRN_CHUNK, HGRN_CHUNK), HGRN_CHUNK)
            qp = q_ref[rows, :]
            oi, qt, kt, eb = _hgrn_intra(qp * _sigmoid(qp), f_ref[rows, :], v_ref[rows, :],
                                         lb_ref[direction:direction + 1, :], dmat_ref, amask_ref, reverse)
            oi_ref[rows, :] = oi
            qt_ref[rows, :] = _bf16(qt)
            kt_ref[rows, :] = _bf16(kt)
            eb_ref[pl.ds(pl.multiple_of(c * 8, 8), 8), :] = jnp.broadcast_to(eb, (8, HGRN_COLS))
            return carry

        lax.fori_loop(0, n_chunks, body, 0, unroll=2)

    def state_step(c):
        rows = pl.ds(pl.multiple_of(c * HGRN_CHUNK, HGRN_CHUNK), HGRN_CHUNK)
        outs = []
        for j, cols in enumerate(head_cols):
            st = st_ref[j]
            outs.append(oi_ref[rows, cols] + _dot_nt(qt_ref[rows, cols], st))
            st_ref[j] = st * eb_ref[pl.ds(pl.multiple_of(c * 8, 8), 1), cols] + _dot_tn(v_ref[rows, cols],
                                                                                       kt_ref[rows, cols])
        return rows, outs

    @pl.when(s < n_tiles)
    def _():
        intra_pass(ff_ref, 0, False)
        base = s * tile_rows

        def body(c, carry):
            rows, outs = state_step(c)
            dst = pl.ds(pl.multiple_of(base + c * HGRN_CHUNK, HGRN_CHUNK), HGRN_CHUNK)
            for j, cols in enumerate(head_cols):
                ofwd_ref[dst, cols] = outs[j]
            return carry

        lax.fori_loop(0, n_chunks, body, 0, unroll=2)

    @pl.when(s >= n_tiles)
    def _():
        intra_pass(fb_ref, 1, True)
        base = (2 * n_tiles - 1 - s) * tile_rows

        def body(ci, carry):
            c = n_chunks - 1 - ci
            rows, outs = state_step(c)
            src = pl.ds(pl.multiple_of(base + c * HGRN_CHUNK, HGRN_CHUNK), HGRN_CHUNK)
            for j, cols in enumerate(head_cols):
                y = _rms(outs[j] + ofwd_ref[src, cols], ng_ref[...])
                g = g_ref[rows, cols]
                o_ref[rows, cols] = y * (g * _sigmoid(g))
            return carry

        lax.fori_loop(0, n_chunks, body, 0, unroll=2)


def _hgrn_tile_rows(T):
    return min(T, 2048)


def _hgrn2_mixer(z, lb_f, lb_b, norm_g):
    B, T, _ = z.shape
    tile_rows = _hgrn_tile_rows(T)
    n_tiles = T // tile_rows
    n_groups = A_HEADS // HGRN_HPS
    lb = jnp.stack([lb_f.reshape(n_groups, HGRN_COLS), lb_b.reshape(n_groups, HGRN_COLS)], axis=1)
    last = n_tiles - 1
    dmat, amask = _hgrn_tables()

    def both(s):
        return jnp.where(s < n_tiles, s, 2 * n_tiles - 1 - s)

    def fwd_only(s):
        return jnp.minimum(s, last)

    def bwd_only(s):
        return jnp.where(s < n_tiles, last, 2 * n_tiles - 1 - s)

    def zspec(part, tile_of):
        return pl.BlockSpec((None, tile_rows, HGRN_COLS),
                            lambda b, h, s: (b, tile_of(s), part * n_groups + h))

    return pl.pallas_call(
        functools.partial(_hgrn_kernel, n_tiles=n_tiles, tile_rows=tile_rows),
        out_shape=jax.ShapeDtypeStruct((B, T, A_WIDTH), jnp.float32),
        grid=(B, n_groups, 2 * n_tiles),
        in_specs=[
            zspec(0, both), zspec(1, fwd_only), zspec(2, bwd_only), zspec(3, both), zspec(4, bwd_only),
            pl.BlockSpec((None, 2, HGRN_COLS), lambda b, h, s: (h, 0, 0)),
            pl.BlockSpec((1, HEAD_DIM), lambda b, h, s: (0, 0)),
            pl.BlockSpec((None,) + dmat.shape[1:], lambda b, h, s: (s // n_tiles, 0, 0)),
            pl.BlockSpec((None,) + amask.shape[1:], lambda b, h, s: (s // n_tiles, 0, 0, 0)),
        ],
        out_specs=pl.BlockSpec((None, tile_rows, HGRN_COLS), lambda b, h, s: (b, bwd_only(s), h)),
        scratch_shapes=[pltpu.VMEM((T, HGRN_COLS), jnp.float32),
                        pltpu.VMEM((HGRN_HPS, HEAD_DIM, HEAD_DIM), jnp.float32),
                        pltpu.VMEM((tile_rows, HGRN_COLS), jnp.float32),
                        pltpu.VMEM((tile_rows, HGRN_COLS), jnp.bfloat16),
                        pltpu.VMEM((tile_rows, HGRN_COLS), jnp.bfloat16),
                        pltpu.VMEM((tile_rows // HGRN_CHUNK * 8, HGRN_COLS), jnp.float32)],
        compiler_params=pltpu.CompilerParams(
            dimension_semantics=("arbitrary", "arbitrary", "arbitrary"),
            vmem_limit_bytes=VMEM_LIMIT_BYTES,
        ),
        name="hgrn2",
    )(z, z, z, z, z, lb.astype(jnp.float32), norm_g.reshape(1, HEAD_DIM).astype(jnp.float32),
      jnp.asarray(dmat, jnp.bfloat16), jnp.asarray(amask, jnp.float32))


NA_KEYS = NA_WIN_R * GRID_W


def _na_bias_table(rpb):
    qc = np.arange(GRID_W)[:, None]
    kc = np.arange(GRID_W)[None, :]
    ws = np.clip(qc - NA_WIN_C // 2, 0, GRID_W - NA_WIN_C)
    col_ok = (kc >= ws) & (kc < ws + NA_WIN_C)
    dc = np.clip(kc - qc + NA_WIN_C - 1, 0, 2 * NA_WIN_C - 2)
    toe = jnp.where(jnp.asarray(col_ok)[None, None], rpb.astype(jnp.float32)[:, :, dc], NEG_INF)
    cases = []
    for c in range(NA_WIN_R):
        cases.append(jnp.concatenate([toe[:, c + j] for j in range(NA_WIN_R)], axis=-1))
    return jnp.stack(cases, axis=1)


def _na_kernel(q_ref, k_ref, v_ref, bias_ref, qg_ref, kg_ref, o_ref, kn_ref, *, rows, tile_grid_rows):
    qt = pl.program_id(2)

    @pl.when(qt == 0)
    def _():
        def kbody(i, carry):
            blk = pl.ds(pl.multiple_of(i * NA_KEYS, NA_KEYS), NA_KEYS)
            kn_ref[blk, :] = _bf16(_rms(k_ref[blk, :], kg_ref[...]))
            return carry

        lax.fori_loop(0, rows * GRID_W // NA_KEYS, kbody, 0)

    def body(rl, carry):
        r = qt * tile_grid_rows + rl
        rs = jnp.clip(r - NA_WIN_R // 2, 0, rows - NA_WIN_R)
        q0 = pl.multiple_of(rl * GRID_W, GRID_W)
        k0 = pl.multiple_of(rs * GRID_W, GRID_W)
        q = _rms(q_ref[pl.ds(q0, GRID_W), :], qg_ref[...]) * (1.0 / np.sqrt(HEAD_DIM))
        s = _dot_nt(q, kn_ref[pl.ds(k0, NA_KEYS), :]) + bias_ref[rs - r + NA_WIN_R - 1]
        m = jnp.max(s, axis=-1, keepdims=True)
        p = jnp.exp(s - m)
        den = jnp.sum(p, axis=-1, keepdims=True)
        o_ref[pl.ds(q0, GRID_W), :] = _dot(p, v_ref[pl.ds(k0, NA_KEYS), :]) / den
        return carry

    lax.fori_loop(0, tile_grid_rows, body, 0, unroll=4)


def _neighborhood_mixer(z, rpb, qnorm_g, knorm_g):
    B, T, _ = z.shape
    rows = T // GRID_W
    assert rows >= NA_WIN_R
    tile_grid_rows = min(rows, 32)
    tq = tile_grid_rows * GRID_W
    bias = _na_bias_table(rpb)
    gq = qnorm_g.reshape(1, HEAD_DIM).astype(jnp.float32)
    gk = knorm_g.reshape(1, HEAD_DIM).astype(jnp.float32)
    return pl.pallas_call(
        functools.partial(_na_kernel, rows=rows, tile_grid_rows=tile_grid_rows),
        out_shape=jax.ShapeDtypeStruct((B, T, B_WIDTH), jnp.float32),
        grid=(B, B_HEADS, T // tq),
        in_specs=[
            pl.BlockSpec((None, tq, HEAD_DIM), lambda b, h, t: (b, t, B_COL0 + h)),
            pl.BlockSpec((None, T, HEAD_DIM), lambda b, h, t: (b, 0, B_COL0 + B_HEADS + h)),
            pl.BlockSpec((None, T, HEAD_DIM), lambda b, h, t: (b, 0, B_COL0 + 2 * B_HEADS + h)),
            pl.BlockSpec((None, NA_WIN_R, GRID_W, NA_KEYS), lambda b, h, t: (h, 0, 0, 0)),
            pl.BlockSpec((1, HEAD_DIM), lambda b, h, t: (0, 0)),
            pl.BlockSpec((1, HEAD_DIM), lambda b, h, t: (0, 0)),
        ],
        out_specs=pl.BlockSpec((None, tq, HEAD_DIM), lambda b, h, t: (b, t, h)),
        scratch_shapes=[pltpu.VMEM((T, HEAD_DIM), jnp.bfloat16)],
        compiler_params=pltpu.CompilerParams(
            dimension_semantics=("arbitrary", "arbitrary", "arbitrary"),
            vmem_limit_bytes=VMEM_LIMIT_BYTES,
        ),
        name="neighborhood_attn",
    )(z, z, z, bias, gq, gk)


DIL_HALF = 64
DIL_QB = 128
assert all(w // (2 * d) == DIL_HALF for w, d in DIL_GROUPS)


def _rope_tables(T):
    half = HEAD_DIM // 2
    inv = 1.0 / (ROPE_THETA ** (jnp.arange(half, dtype=jnp.float32) * 2.0 / HEAD_DIM))
    ang = jnp.arange(T, dtype=jnp.float32)[:, None] * inv[None, :]
    cos, sin = jnp.cos(ang), jnp.sin(ang)
    return jnp.concatenate([cos, cos], axis=-1), jnp.concatenate([-sin, sin], axis=-1)


def _dil_kernel(q_ref, k_ref, v_ref, qcos_ref, qsin_ref, kcos_ref, ksin_ref, qg_ref, kg_ref,
                o_ref, lse_ref, kn_ref, *, L, Lq, KW, QB, KB):
    qi = pl.program_id(3)

    def norm_rope(x, g, cos, sin):
        xn = _rms(x, g)
        return xn * cos + pltpu.roll(xn, HEAD_DIM // 2, 1) * sin

    @pl.when(qi == 0)
    def _():
        def kbody(i, carry):
            r = pl.ds(pl.multiple_of(i * KB, KB), KB)
            kn_ref[r, :] = _bf16(norm_rope(k_ref[r, :], kg_ref[...], kcos_ref[r, :], ksin_ref[r, :]))
            return carry

        lax.fori_loop(0, L // KB, kbody, 0)

    def body(i, carry):
        q0l = pl.multiple_of(i * QB, QB)
        q0 = qi * Lq + q0l
        kb = pl.multiple_of(jnp.clip(q0 - DIL_HALF, 0, L - KW), DIL_HALF)
        rows = pl.ds(q0l, QB)
        q = norm_rope(q_ref[rows, :], qg_ref[...], qcos_ref[rows, :], qsin_ref[rows, :])
        s = _dot_nt(q * (1.0 / np.sqrt(HEAD_DIM)), kn_ref[pl.ds(kb, KW), :])
        qpos = q0 + lax.broadcasted_iota(jnp.int32, (QB, KW), 0)
        kpos = kb + lax.broadcasted_iota(jnp.int32, (QB, KW), 1)
        s = jnp.where(jnp.abs(qpos - kpos) <= DIL_HALF, s, NEG_INF)
        m = jnp.max(s, axis=-1, keepdims=True)
        p = jnp.exp(s - m)
        den = jnp.sum(p, axis=-1, keepdims=True)
        o_ref[rows, :] = _dot(p, v_ref[pl.ds(kb, KW), :]) / den
        lse_ref[rows, :] = jnp.broadcast_to(m + jnp.log(den), (QB, HEAD_DIM))
        return carry

    lax.fori_loop(0, Lq // QB, body, 0, unroll=2 if Lq // QB >= 2 else 1)


C_GROUP_COLS = 3 * C_HEADS_PER_GROUP * HEAD_DIM


def _c_proj_kernel(x_ref, g_ref, w_ref, o0_ref, o1_ref, o2_ref, xn_ref, res_ref, *, tm):
    gi = pl.program_id(1)

    @pl.when(gi == 0)
    def _():
        xn_ref[...] = _bf16(_rms(x_ref[...], g_ref[...]))

    res = jnp.dot(xn_ref[...], w_ref[...], preferred_element_type=jnp.float32)
    for g, ((_, d), o_ref) in enumerate(zip(DIL_GROUPS, (o0_ref, o1_ref, o2_ref))):
        @pl.when(gi == g)
        def _(d=d, o_ref=o_ref):
            if d == 1:
                o_ref[...] = res
            else:
                for c in range(C_GROUP_COLS // LANES):
                    res_ref[c] = res[:, c * LANES:(c + 1) * LANES]
                for j in range(d):
                    for c in range(C_GROUP_COLS // LANES):
                        c0 = j * C_GROUP_COLS + c * LANES
                        o_ref[:, c0:c0 + LANES] = res_ref[c, pl.ds(j, tm // d, stride=d), :]


def _c_proj(hf, gain, w_c, *, tm=512):
    n, k = hf.shape
    hp = C_HEADS_PER_GROUP
    cols = []
    for g in range(len(DIL_GROUPS)):
        for part in range(3):
            c0 = part * C_WIDTH + g * hp * HEAD_DIM
            cols.append(w_c[:, c0:c0 + hp * HEAD_DIM])
    w_perm = jnp.concatenate(cols, axis=1).astype(jnp.bfloat16)
    return pl.pallas_call(
        functools.partial(_c_proj_kernel, tm=tm),
        out_shape=tuple(jax.ShapeDtypeStruct((n // d, d * C_GROUP_COLS), jnp.float32) for _, d in DIL_GROUPS),
        grid=(n // tm, len(DIL_GROUPS)),
        in_specs=[pl.BlockSpec((tm, k), lambda i, g: (i, 0)),
                  pl.BlockSpec((1, k), lambda i, g: (0, 0)),
                  pl.BlockSpec((k, C_GROUP_COLS), lambda i, g: (0, g))],
        out_specs=tuple(pl.BlockSpec((tm // d, d * C_GROUP_COLS), lambda i, g: (i, 0)) for _, d in DIL_GROUPS),
        scratch_shapes=[pltpu.VMEM((tm, k), jnp.bfloat16),
                        pltpu.VMEM((C_GROUP_COLS // LANES, tm, LANES), jnp.float32)],
        compiler_params=pltpu.CompilerParams(dimension_semantics=("arbitrary", "arbitrary"),
                                             vmem_limit_bytes=VMEM_LIMIT_BYTES),
        name="c_proj",
    )(hf, gain.reshape(1, k).astype(jnp.float32), w_perm)


def _dilated_group(zg, B, T, d, qnorm_g, knorm_g, cos2, sin2):
    hp = C_HEADS_PER_GROUP
    L = T // d
    Lq = min(L, 1024)
    QB = min(DIL_QB, Lq)
    KW = min(2 * DIL_QB, L)
    KB = min(L, 512)
    zv = zg.reshape(B, L, d * C_GROUP_COLS)
    cosv = cos2.reshape(L, d * HEAD_DIM)
    sinv = sin2.reshape(L, d * HEAD_DIM)
    gq = qnorm_g.reshape(1, HEAD_DIM).astype(jnp.float32)
    gk = knorm_g.reshape(1, HEAD_DIM).astype(jnp.float32)

    def zcol(part):
        return lambda b, j, h, qi: (j * 3 + part) * hp + h

    out_sds = jax.ShapeDtypeStruct((B, L, d * hp * HEAD_DIM), jnp.float32)
    out_spec = pl.BlockSpec((None, Lq, HEAD_DIM), lambda b, j, h, qi: (b, qi, j * hp + h))
    o, lse = pl.pallas_call(
        functools.partial(_dil_kernel, L=L, Lq=Lq, KW=KW, QB=QB, KB=KB),
        out_shape=(out_sds, out_sds),
        grid=(B, d, hp, L // Lq),
        in_specs=[
            pl.BlockSpec((None, Lq, HEAD_DIM), lambda b, j, h, qi: (b, qi, zcol(0)(b, j, h, qi))),
            pl.BlockSpec((None, L, HEAD_DIM), lambda b, j, h, qi: (b, 0, zcol(1)(b, j, h, qi))),
            pl.BlockSpec((None, L, HEAD_DIM), lambda b, j, h, qi: (b, 0, zcol(2)(b, j, h, qi))),
            pl.BlockSpec((Lq, HEAD_DIM), lambda b, j, h, qi: (qi, j)),
            pl.BlockSpec((Lq, HEAD_DIM), lambda b, j, h, qi: (qi, j)),
            pl.BlockSpec((L, HEAD_DIM), lambda b, j, h, qi: (0, j)),
            pl.BlockSpec((L, HEAD_DIM), lambda b, j, h, qi: (0, j)),
            pl.BlockSpec((1, HEAD_DIM), lambda b, j, h, qi: (0, 0)),
            pl.BlockSpec((1, HEAD_DIM), lambda b, j, h, qi: (0, 0)),
        ],
        out_specs=(out_spec, out_spec),
        scratch_shapes=[pltpu.VMEM((L, HEAD_DIM), jnp.bfloat16)],
        compiler_params=pltpu.CompilerParams(
            dimension_semantics=("arbitrary", "arbitrary", "arbitrary", "arbitrary"),
            vmem_limit_bytes=VMEM_LIMIT_BYTES,
        ),
        name=f"dilated_attn_d{d}",
    )(zv, zv, zv, cosv, sinv, cosv, sinv, gq, gk)
    return o.reshape(B * L, d * hp * HEAD_DIM), lse.reshape(B * L, d * hp * HEAD_DIM)


def _dilated_mixer(zgs, B, T, qnorm_g, knorm_g):
    cos2, sin2 = _rope_tables(T)
    outs, lses = [], []
    for zg, (_, d) in zip(zgs, DIL_GROUPS):
        o, lse = _dilated_group(zg, B, T, d, qnorm_g, knorm_g, cos2, sin2)
        outs.append(o)
        lses.append(lse)
    return outs, lses


def _out_proj_kernel(h_ref, a_ref, b_ref, o0_ref, o1_ref, o2_ref, l0_ref, l1_ref, l2_ref, w_ref,
                     out_ref, mix_ref, stage_ref, *, tm):
    gw = C_HEADS_PER_GROUP * HEAD_DIM

    def token_major(src_ref, d, slot):
        if d == 1:
            return src_ref[...]
        nc = gw // LANES
        for j in range(d):
            for c in range(nc):
                c0 = j * gw + c * LANES
                stage_ref[slot * nc + c, pl.ds(j, tm // d, stride=d), :] = src_ref[:, c0:c0 + LANES]
        return jnp.concatenate([stage_ref[slot * nc + c] for c in range(nc)], axis=1)

    @pl.when(pl.program_id(1) == 0)
    def _():
        mix_ref[:, :A_WIDTH] = _bf16(a_ref[...])
        mix_ref[:, A_WIDTH:A_WIDTH + B_WIDTH] = _bf16(b_ref[...])
        dils = [d for _, d in DIL_GROUPS]
        lses = [token_major(l_ref, d, g) for g, (l_ref, d) in enumerate(zip((l0_ref, l1_ref, l2_ref), dils))]
        m = jnp.maximum(jnp.maximum(lses[0], lses[1]), lses[2])
        es = [jnp.exp(l - m) for l in lses]
        inv = 1.0 / (es[0] + es[1] + es[2])
        for g, (o_ref, d) in enumerate(zip((o0_ref, o1_ref, o2_ref), dils)):
            c0 = A_WIDTH + B_WIDTH + g * gw
            mix_ref[:, c0:c0 + gw] = _bf16(token_major(o_ref, d, len(dils) + g) * (es[g] * inv))

    out_ref[...] = h_ref[...] + jnp.dot(mix_ref[...], w_ref[...], preferred_element_type=jnp.float32)


def _out_proj(hf, a_out, b_out, c_outs, c_lses, w_out, *, tm=512, tn=512):
    n = hf.shape[0]
    gw = C_HEADS_PER_GROUP * HEAD_DIM
    row = lambda width: pl.BlockSpec((tm, width), lambda i, j: (i, 0))
    dil = [pl.BlockSpec((tm // d, d * gw), lambda i, j: (i, 0)) for _, d in DIL_GROUPS]
    return pl.pallas_call(
        functools.partial(_out_proj_kernel, tm=tm),
        out_shape=jax.ShapeDtypeStruct((n, D_MODEL), jnp.float32),
        grid=(n // tm, D_MODEL // tn),
        in_specs=[pl.BlockSpec((tm, tn), lambda i, j: (i, j)), row(A_WIDTH), row(B_WIDTH)]
        + dil + dil + [pl.BlockSpec((D_MODEL, tn), lambda i, j: (0, j))],
        out_specs=pl.BlockSpec((tm, tn), lambda i, j: (i, j)),
        scratch_shapes=[pltpu.VMEM((tm, D_MODEL), jnp.bfloat16),
                        pltpu.VMEM((2 * len(DIL_GROUPS) * gw // LANES, tm, LANES), jnp.float32)],
        compiler_params=pltpu.CompilerParams(
            dimension_semantics=("arbitrary", "arbitrary"),
            vmem_limit_bytes=VMEM_LIMIT_BYTES,
        ),
        name="out_proj",
    )(hf, a_out.reshape(n, A_WIDTH), b_out.reshape(n, B_WIDTH), *c_outs, *c_lses,
      w_out.astype(jnp.bfloat16))


def _ple_kernel(hrow_ref, h_ref, p_ref, gg_ref, pg_ref, wp_ref, wg_ref, out_ref, xn_ref, e_ref, *, tn):
    j = pl.program_id(1)

    @pl.when(j == 0)
    def _():
        xn_ref[...] = _bf16(_rms(hrow_ref[...], gg_ref[...]))
        e = _rms(jnp.dot(_bf16(p_ref[...]), wp_ref[...], preferred_element_type=jnp.float32), pg_ref[...])
        for jj in range(D_MODEL // tn):
            e_ref[jj] = e[:, jj * tn:(jj + 1) * tn]

    gate = _sigmoid(jnp.dot(xn_ref[...], wg_ref[...], preferred_element_type=jnp.float32))
    out_ref[...] = h_ref[...] + gate * e_ref[j]


def _ple(hf, p, w_ple, ple_norm_g, gate_norm_g, w_ple_gate, *, tm=512, tn=512):
    n = hf.shape[0]
    pd = p.shape[1]
    vec = lambda g: g.reshape(1, D_MODEL).astype(jnp.float32)
    return pl.pallas_call(
        functools.partial(_ple_kernel, tn=tn),
        out_shape=jax.ShapeDtypeStruct((n, D_MODEL), jnp.float32),
        grid=(n // tm, D_MODEL // tn),
        in_specs=[
            pl.BlockSpec((tm, D_MODEL), lambda i, j: (i, 0)),
            pl.BlockSpec((tm, tn), lambda i, j: (i, j)),
            pl.BlockSpec((tm, pd), lambda i, j: (i, 0)),
            pl.BlockSpec((1, D_MODEL), lambda i, j: (0, 0)),
            pl.BlockSpec((1, D_MODEL), lambda i, j: (0, 0)),
            pl.BlockSpec((pd, D_MODEL), lambda i, j: (0, 0)),
            pl.BlockSpec((D_MODEL, tn), lambda i, j: (0, j)),
        ],
        out_specs=pl.BlockSpec((tm, tn), lambda i, j: (i, j)),
        scratch_shapes=[pltpu.VMEM((tm, D_MODEL), jnp.bfloat16),
                        pltpu.VMEM((D_MODEL // tn, tm, tn), jnp.float32)],
        compiler_params=pltpu.CompilerParams(
            dimension_semantics=("arbitrary", "arbitrary"),
            vmem_limit_bytes=VMEM_LIMIT_BYTES,
        ),
        name="ple",
    )(hf, hf, p, vec(gate_norm_g), vec(ple_norm_g), w_ple.astype(jnp.bfloat16),
      w_ple_gate.astype(jnp.bfloat16))


LANES = 128
FFN_ROWS = 1024
HALF_D = D_MODEL // 2
COMBINE_ROWS = 512


def _pack_bf16_pairs(lo, hi):
    bits = lambda v: lax.bitcast_convert_type(_bf16(v).astype(jnp.float32), jnp.uint32)
    return (bits(lo) >> 16) | (bits(hi) & jnp.uint32(0xFFFF0000))


def _unpack_bf16_pairs(words):
    return (lax.bitcast_convert_type(words << 16, jnp.float32),
            lax.bitcast_convert_type(words & jnp.uint32(0xFFFF0000), jnp.float32))


def _router_kernel(h_ref, g_ref, wr_ref, xn_ref, aff_ref):
    xn = _rms(h_ref[...], g_ref[...])
    xn_ref[...] = _bf16(xn)
    logits = jnp.dot(xn, wr_ref[...], preferred_element_type=jnp.float32, precision=lax.Precision.HIGHEST)
    lane = lax.broadcasted_iota(jnp.int32, logits.shape, 1)
    logits = jnp.where(lane < N_EXPERTS, logits, NEG_INF)
    e = jnp.exp(logits - jnp.max(logits, axis=-1, keepdims=True))
    aff_ref[...] = e / jnp.sum(e, axis=-1, keepdims=True)


def _router(hf, norm_g, w_router, *, tm=512):
    n = hf.shape[0]
    wr = jnp.zeros((D_MODEL, LANES), jnp.float32).at[:, :N_EXPERTS].set(w_router.astype(jnp.float32))
    return pl.pallas_call(
        _router_kernel,
        out_shape=(jax.ShapeDtypeStruct((n, D_MODEL), jnp.bfloat16),
                   jax.ShapeDtypeStruct((n, LANES), jnp.float32)),
        grid=(n // tm,),
        in_specs=[pl.BlockSpec((tm, D_MODEL), lambda i: (i, 0)),
                  pl.BlockSpec((1, D_MODEL), lambda i: (0, 0)),
                  pl.BlockSpec((D_MODEL, LANES), lambda i: (0, 0))],
        out_specs=(pl.BlockSpec((tm, D_MODEL), lambda i: (i, 0)),
                   pl.BlockSpec((tm, LANES), lambda i: (i, 0))),
        compiler_params=pltpu.CompilerParams(dimension_semantics=("arbitrary",),
                                             vmem_limit_bytes=VMEM_LIMIT_BYTES),
        name="router",
    )(hf, norm_g.reshape(1, D_MODEL).astype(jnp.float32), wr)


def _expert_kernel(dest_hbm, x_ref, g_ref, wg_ref, wu_ref, wd_ref, y_hbm, ybuf, dbuf, ysem, dsem,
                   *, n_steps, steps_per_expert):
    step = pl.program_id(0) * steps_per_expert + pl.program_id(1)
    slot = step % 2

    def dest_copy(s, sl):
        return pltpu.make_async_copy(dest_hbm.at[s], dbuf.at[sl], dsem.at[sl])

    def scatter_wait(sl):
        pltpu.make_async_copy(ybuf.at[sl], ybuf.at[sl], ysem.at[sl]).wait()

    @pl.when(step == 0)
    def _():
        dest_copy(0, 0).start()

    @pl.when(step + 1 < n_steps)
    def _():
        dest_copy(step + 1, 1 - slot).start()

    @pl.when(step >= 2)
    def _():
        scatter_wait(slot)

    x = x_ref[...]
    gate = jnp.dot(x, wg_ref[...], preferred_element_type=jnp.float32)
    hdn = gate * _sigmoid(gate) * jnp.dot(x, wu_ref[...], preferred_element_type=jnp.float32)
    y = jnp.dot(_bf16(hdn), wd_ref[...], preferred_element_type=jnp.float32)
    y = y * jnp.concatenate([g_ref[...]] * (D_MODEL // LANES), axis=1)
    ybuf[slot] = _pack_bf16_pairs(y[:, :HALF_D], y[:, HALF_D:])

    dest_copy(step, slot).wait()
    for r in range(FFN_ROWS):
        pltpu.make_async_copy(ybuf.at[slot, pl.ds(r, 1), :],
                              y_hbm.at[pl.ds(dbuf[slot, r], 1), :], ysem.at[slot]).start()

    @pl.when(step == n_steps - 1)
    def _():
        scatter_wait(slot)
        if n_steps >= 2:
            scatter_wait(1 - slot)


def _expert_ffn(xe, gates_rep, dest, w_gate, w_up, w_down, n_tokens):
    rows, d = xe.shape
    cap = rows // N_EXPERTS
    f = w_gate.shape[-1]
    spe = cap // FFN_ROWS
    n_steps = N_EXPERTS * spe
    return pl.pallas_call(
        functools.partial(_expert_kernel, n_steps=n_steps, steps_per_expert=spe),
        out_shape=jax.ShapeDtypeStruct((N_EXPERTS * n_tokens, HALF_D), jnp.uint32),
        grid=(N_EXPERTS, spe),
        in_specs=[
            pl.BlockSpec(memory_space=pl.ANY),
            pl.BlockSpec((FFN_ROWS, d), lambda e, t: (e * spe + t, 0)),
            pl.BlockSpec((FFN_ROWS, LANES), lambda e, t: (e * spe + t, 0)),
            pl.BlockSpec((None, d, f), lambda e, t: (e, 0, 0), pipeline_mode=pl.Buffered(1)),
            pl.BlockSpec((None, d, f), lambda e, t: (e, 0, 0), pipeline_mode=pl.Buffered(1)),
            pl.BlockSpec((None, f, d), lambda e, t: (e, 0, 0), pipeline_mode=pl.Buffered(1)),
        ],
        out_specs=pl.BlockSpec(memory_space=pl.ANY),
        scratch_shapes=[pltpu.VMEM((2, FFN_ROWS, HALF_D), jnp.uint32),
                        pltpu.SMEM((2, FFN_ROWS), jnp.int32),
                        pltpu.SemaphoreType.DMA((2,)),
                        pltpu.SemaphoreType.DMA((2,))],
        compiler_params=pltpu.CompilerParams(dimension_semantics=("arbitrary", "arbitrary"),
                                             vmem_limit_bytes=VMEM_LIMIT_BYTES),
        name="expert_ffn",
    )(dest.reshape(n_steps, FFN_ROWS), xe, gates_rep, w_gate.astype(jnp.bfloat16),
      w_up.astype(jnp.bfloat16), w_down.astype(jnp.bfloat16))


COMBINE_RANKS = 4


def _combine_kernel(tmax_ref, h_ref, cnt_ref, *refs):
    y_refs, o_ref = refs[:COMBINE_RANKS], refs[COMBINE_RANKS]
    i, rg = pl.program_id(0), pl.program_id(1)

    @pl.when(rg == 0)
    def _():
        o_ref[...] = h_ref[...]

    for q, y_ref in enumerate(y_refs):
        r = rg * COMBINE_RANKS + q

        @pl.when(r < tmax_ref[i])
        def _(r=r, y_ref=y_ref):
            chosen = jnp.concatenate([cnt_ref[...]] * (HALF_D // LANES), axis=1) > r
            lo, hi = _unpack_bf16_pairs(y_ref[...])
            o_ref[:, :HALF_D] += jnp.where(chosen, lo, 0.0)
            o_ref[:, HALF_D:] += jnp.where(chosen, hi, 0.0)


def _combine(hf, slabs, cnt_rep, tile_max):
    n, d = hf.shape
    tm = COMBINE_ROWS
    slabs = slabs.reshape(N_EXPERTS, n, HALF_D)

    def slab_spec(q):
        return pl.BlockSpec((None, tm, HALF_D), lambda i, rg, tmax: (
            jnp.minimum(rg * COMBINE_RANKS + q, jnp.maximum(tmax[i], 1) - 1), i, 0))

    return pl.pallas_call(
        _combine_kernel,
        out_shape=jax.ShapeDtypeStruct((n, d), jnp.float32),
        grid_spec=pltpu.PrefetchScalarGridSpec(
            num_scalar_prefetch=1,
            grid=(n // tm, N_EXPERTS // COMBINE_RANKS),
            in_specs=[pl.BlockSpec((tm, d), lambda i, rg, tmax: (i, 0)),
                      pl.BlockSpec((tm, LANES), lambda i, rg, tmax: (i, 0))]
            + [slab_spec(q) for q in range(COMBINE_RANKS)],
            out_specs=pl.BlockSpec((tm, d), lambda i, rg, tmax: (i, 0)),
        ),
        compiler_params=pltpu.CompilerParams(dimension_semantics=("arbitrary", "arbitrary"),
                                             vmem_limit_bytes=VMEM_LIMIT_BYTES),
        name="ffn_combine",
    )(tile_max, hf, cnt_rep, *([slabs] * COMBINE_RANKS))


def _expert_choice_ffn(hf, norm_g, w_router, w_gate, w_up, w_down):
    n = hf.shape[0]
    cap = (EC_CAPACITY_FACTOR * n) // N_EXPERTS
    xn, aff = _router(hf, norm_g, w_router)
    gates, idx = lax.top_k(aff[:, :N_EXPERTS].T, cap)
    xe = xn[idx.reshape(-1)]
    chosen = jnp.zeros((N_EXPERTS, n), jnp.int32).at[jnp.arange(N_EXPERTS)[:, None], idx].set(1)
    rank = jnp.cumsum(chosen, axis=0) - chosen
    dest = jnp.take_along_axis(rank, idx, axis=1) * n + idx
    cnt = jnp.sum(chosen, axis=0)
    slabs = _expert_ffn(xe, jnp.broadcast_to(gates.reshape(-1, 1), (N_EXPERTS * cap, LANES)),
                        dest.reshape(-1).astype(jnp.int32), w_gate, w_up, w_down, n)
    cnt_rep = jnp.broadcast_to(cnt[:, None], (n, LANES)).astype(jnp.int32)
    tile_max = jnp.max(cnt.reshape(n // COMBINE_ROWS, COMBINE_ROWS), axis=1).astype(jnp.int32)
    return _combine(hf, slabs, cnt_rep, tile_max)


def _layer(h, p_i, lb_f, lb_b, norm1_g, w_in, a_norm_g, b_qnorm_g, b_knorm_g, b_rpb,
           c_qnorm_g, c_knorm_g, w_out, norm2_g, w_router, w_gate, w_up, w_down,
           w_ple, ple_norm_g, gate_norm_g, w_ple_gate):
    B, T, _ = h.shape
    N = B * T
    hf = h.reshape(N, D_MODEL)
    ab_width = IN_WIDTH - 3 * C_WIDTH
    z_ab = _matmul(hf, w_in[:, :ab_width], norm1_g, tn=768).reshape(B, T, ab_width)
    z_c = _c_proj(hf, norm1_g, w_in[:, ab_width:])
    a_out = _hgrn2_mixer(z_ab, lb_f, lb_b, a_norm_g)
    b_out = _neighborhood_mixer(z_ab, b_rpb, b_qnorm_g, b_knorm_g)
    c_outs, c_lses = _dilated_mixer(z_c, B, T, c_qnorm_g, c_knorm_g)
    hf = _out_proj(hf, a_out, b_out, c_outs, c_lses, w_out)
    hf = _expert_choice_ffn(hf, norm2_g, w_router, w_gate, w_up, w_down)
    return _ple(hf, p_i.reshape(N, -1), w_ple, ple_norm_g, gate_norm_g, w_ple_gate).reshape(B, T, D_MODEL)


def _trunk(h, p, lb, weights):
    (norm1_g, w_in, a_norm_g, b_qnorm_g, b_knorm_g, b_rpb, c_qnorm_g, c_knorm_g, w_out,
     norm2_g, w_router, w_gate, w_up, w_down, w_ple, ple_norm_g, gate_norm_g, w_ple_gate) = weights
    for i in range(DEPTH):
        h = _layer(h, p[i], lb[0, i], lb[1, i], norm1_g[i], w_in[i], a_norm_g[i], b_qnorm_g[i],
                   b_knorm_g[i], b_rpb[i], c_qnorm_g[i], c_knorm_g[i], w_out[i], norm2_g[i],
                   w_router[i], w_gate[i], w_up[i], w_down[i], w_ple[i], ple_norm_g[i],
                   gate_norm_g[i], w_ple_gate[i])
    return h


def kernel(x_prompt, x_sample, p_prompt, p_sample, norm1_g, w_in, lb_logits, a_norm_g,
           b_qnorm_g, b_knorm_g, b_rpb, c_qnorm_g, c_knorm_g, w_out, norm2_g, w_router,
           w_gate, w_up, w_down, w_ple, ple_norm_g, gate_norm_g, w_ple_gate):
    pr = jax.nn.softmax(lb_logits.astype(jnp.float32), axis=1)
    lb = jnp.cumsum(pr, axis=1) - pr[:, :1]
    weights = (norm1_g, w_in, a_norm_g, b_qnorm_g, b_knorm_g, b_rpb, c_qnorm_g, c_knorm_g, w_out,
               norm2_g, w_router, w_gate, w_up, w_down, w_ple, ple_norm_g, gate_norm_g, w_ple_gate)
    y_prompt = _trunk(x_prompt, p_prompt, lb, weights)
    y_sample = _trunk(x_sample, p_sample, lb, weights)
    return (y_prompt, y_sample)
```

```python
import functools

import jax
import jax.numpy as jnp
import numpy as np
from jax import lax
from jax.experimental import pallas as pl
from jax.experimental.pallas import tpu as pltpu

D_MODEL = 2048
DEPTH = 4
HEAD_DIM = 128
A_HEADS = 6
B_HEADS = 4
C_HEADS = 6
A_WIDTH = A_HEADS * HEAD_DIM
B_WIDTH = B_HEADS * HEAD_DIM
C_WIDTH = C_HEADS * HEAD_DIM
IN_WIDTH = 5 * A_WIDTH + 3 * B_WIDTH + 3 * C_WIDTH
HGRN_CHUNK = 128
GRID_W = 64
NA_WIN_R = 8
NA_WIN_C = 16
DIL_GROUPS = ((128, 1), (512, 4), (2048, 16))
C_HEADS_PER_GROUP = C_HEADS // len(DIL_GROUPS)
ROPE_THETA = 10000.0
N_EXPERTS = 16
EC_CAPACITY_FACTOR = 2
EPS = 1e-6

VMEM_LIMIT_BYTES = 56 * 1024 * 1024
NEG_INF = -1e30
Z_COLS = IN_WIDTH // HEAD_DIM
B_COL0 = 5 * A_HEADS
C_COL0 = B_COL0 + 3 * B_HEADS


def _sigmoid(x):
    return 1.0 / (1.0 + jnp.exp(-x))


def _bf16(x):
    return x.astype(jnp.bfloat16)


def _dot(a, b):
    return jnp.dot(_bf16(a), _bf16(b), preferred_element_type=jnp.float32)


def _dot_nt(a, b):
    return lax.dot_general(_bf16(a), _bf16(b), (((1,), (1,)), ((), ())),
                           preferred_element_type=jnp.float32)


def _dot_tn(a, b):
    return lax.dot_general(_bf16(a), _bf16(b), (((0,), (0,)), ((), ())),
                           preferred_element_type=jnp.float32)


def _rms(x, g):
    return x * lax.rsqrt(jnp.mean(x * x, axis=-1, keepdims=True) + EPS) * g


def _norm_matmul_kernel(x_ref, g_ref, w_ref, o_ref, xn_ref, *, normalize):
    @pl.when(pl.program_id(1) == 0)
    def _():
        x = x_ref[...]
        if normalize:
            x = _rms(x, g_ref[...])
        xn_ref[...] = x.astype(jnp.bfloat16)

    o_ref[...] = jnp.dot(xn_ref[...], w_ref[...], preferred_element_type=jnp.float32)


def _matmul(x, w, gain=None, *, tm=1024, tn=512):
    n, k = x.shape
    m = w.shape[1]
    tm = min(tm, n)
    tn = min(tn, m)
    assert n % tm == 0 and m % tn == 0
    normalize = gain is not None
    g = (gain if normalize else jnp.ones((k,), jnp.float32)).reshape(1, k).astype(jnp.float32)
    return pl.pallas_call(
        functools.partial(_norm_matmul_kernel, normalize=normalize),
        out_shape=jax.ShapeDtypeStruct((n, m), jnp.float32),
        grid=(n // tm, m // tn),
        in_specs=[
            pl.BlockSpec((tm, k), lambda i, j: (i, 0)),
            pl.BlockSpec((1, k), lambda i, j: (0, 0)),
            pl.BlockSpec((k, tn), lambda i, j: (0, j)),
        ],
        out_specs=pl.BlockSpec((tm, tn), lambda i, j: (i, j)),
        scratch_shapes=[pltpu.VMEM((tm, k), jnp.bfloat16)],
        compiler_params=pltpu.CompilerParams(
            dimension_semantics=("arbitrary", "arbitrary"),
            vmem_limit_bytes=VMEM_LIMIT_BYTES,
        ),
        name="norm_matmul" if normalize else "matmul",
    )(x, g, w.astype(jnp.bfloat16))


HGRN_LEVELS = tuple(HGRN_CHUNK >> (i + 1) for i in range(HGRN_CHUNK.bit_length() - 1))
HGRN_NLEV = len(HGRN_LEVELS)
HGRN_HPS = 2
HGRN_COLS = HGRN_HPS * HEAD_DIM


def _hgrn_tables():
    C = HGRN_CHUNK
    t = np.arange(C)[:, None]
    u = np.arange(C)[None, :]
    dmats, amasks = [], []
    for reverse in (False, True):
        cum = ((u >= t) if reverse else (u <= t)).astype(np.float32)
        blocks, masks = [], []
        for m in HGRN_LEVELS:
            pos = np.arange(C) % (2 * m)
            ref = np.arange(C) - pos + (m if reverse else m - 1)
            blocks.append(cum - cum[ref])
            later = (pos < m) if reverse else (pos >= m)
            same = (t // (2 * m)) == (u // (2 * m))
            masks.append((same & later[:, None] & ~later[None, :]).astype(np.float32))
        last = 0 if reverse else C - 1
        blocks += [cum, cum[last:last + 1] - cum]
        masks.append(np.eye(C, dtype=np.float32))
        assert np.array_equal(sum(masks), cum)
        dmats.append(np.tile(np.concatenate(blocks, axis=0), (1, 2)))
        amasks.append(np.stack(masks, axis=0))
    return np.stack(dmats, axis=0), np.stack(amasks, axis=0)


def _hgrn_intra(q, x, v, lb, dmat, amask, reverse):
    C = HGRN_CHUNK
    f = lb + (1.0 - lb) * _sigmoid(x)
    lf = jnp.log(f)
    k = 1.0 - f
    hi = _bf16(lf)
    mid = _bf16(lf - hi.astype(jnp.float32))
    dall = jnp.dot(dmat[...], jnp.concatenate([hi, mid], axis=0),
                   preferred_element_type=jnp.float32)

    def stack(a):
        return jnp.concatenate([a[:, j * HEAD_DIM:(j + 1) * HEAD_DIM] for j in range(HGRN_HPS)], axis=0)

    def head_blocks(p):
        return [p[j * C:(j + 1) * C, j * C:(j + 1) * C] for j in range(HGRN_HPS)]

    qs, ks = stack(q), stack(k)
    row = lax.broadcasted_iota(jnp.int32, (HGRN_HPS * C, HEAD_DIM), 0)
    a = [amask[HGRN_NLEV] * p for p in head_blocks(_dot_nt(qs, ks))]
    for l, m in enumerate(HGRN_LEVELS):
        later = ((row & m) == 0) if reverse else ((row & m) != 0)
        xh = _bf16(jnp.where(later, qs, ks) * jnp.exp(-jnp.abs(stack(dall[l * C:(l + 1) * C]))))
        pairs = lax.dot_general(xh, xh, (((1,), (1,)), ((), ())), preferred_element_type=jnp.float32)
        a = [a_j + amask[l] * p for a_j, p in zip(a, head_blocks(pairs))]
    oi = jnp.concatenate([_dot(a[j], v[:, j * HEAD_DIM:(j + 1) * HEAD_DIM]) for j in range(HGRN_HPS)],
                         axis=1)
    b = dall[HGRN_NLEV * C:(HGRN_NLEV + 1) * C]
    b_rest = dall[(HGRN_NLEV + 1) * C:(HGRN_NLEV + 2) * C]
    b_last = b[0:1, :] if reverse else b[C - 1:C, :]
    return oi, q * jnp.exp(b), k * jnp.exp(b_rest), jnp.exp(b_last)


def _hgrn_kernel(q_ref, ff_ref, fb_ref, v_ref, g_ref, lb_ref, ng_ref, dmat_ref, amask_ref, o_ref,
                 ofwd_ref, st_ref, oi_ref, qt_ref, kt_ref, eb_ref, *, n_tiles, tile_rows):
    s = pl.program_id(2)
    n_chunks = tile_rows // HGRN_CHUNK
    head_cols = [slice(j * HEAD_DIM, (j + 1) * HEAD_DIM) for j in range(HGRN_HPS)]

    @pl.when((s == 0) | (s == n_tiles))
    def _():
        st_ref[...] = jnp.zeros_like(st_ref)

    def intra_pass(f_ref, direction, reverse):
        def body(c, carry):
            rows = pl.ds(pl.multiple_of(c * HGRN_CHUNK, HGRN_CHUNK), HGRN_CHUNK)
            qp = q_ref[rows, :]
            oi, qt, kt, eb = _hgrn_intra(qp * _sigmoid(qp), f_ref[rows, :], v_ref[rows, :],
                                         lb_ref[direction:direction + 1, :], dmat_ref, amask_ref, reverse)
            oi_ref[rows, :] = oi
            qt_ref[rows, :] = _bf16(qt)
            kt_ref[rows, :] = _bf16(kt)
            eb_ref[pl.ds(pl.multiple_of(c * 8, 8), 8), :] = jnp.broadcast_to(eb, (8, HGRN_COLS))
            return carry

        lax.fori_loop(0, n_chunks, body, 0, unroll=2)

    def state_step(c):
        rows = pl.ds(pl.multiple_of(c * HGRN_CHUNK, HGRN_CHUNK), HGRN_CHUNK)
        outs = []
        for j, cols in enumerate(head_cols):
            st = st_ref[j]
            outs.append(oi_ref[rows, cols] + _dot_nt(qt_ref[rows, cols], st))
            st_ref[j] = st * eb_ref[pl.ds(pl.multiple_of(c * 8, 8), 1), cols] + _dot_tn(v_ref[rows, cols],
                                                                                       kt_ref[rows, cols])
        return rows, outs

    @pl.when(s < n_tiles)
    def _():
        intra_pass(ff_ref, 0, False)
        base = s * tile_rows

        def body(c, carry):
            rows, outs = state_step(c)
            dst = pl.ds(pl.multiple_of(base + c * HGRN_CHUNK, HGRN_CHUNK), HGRN_CHUNK)
            for j, cols in enumerate(head_cols):
                ofwd_ref[dst, cols] = outs[j]
            return carry

        lax.fori_loop(0, n_chunks, body, 0, unroll=2)

    @pl.when(s >= n_tiles)
    def _():
        intra_pass(fb_ref, 1, True)
        base = (2 * n_tiles - 1 - s) * tile_rows

        def body(ci, carry):
            c = n_chunks - 1 - ci
            rows, outs = state_step(c)
            src = pl.ds(pl.multiple_of(base + c * HGRN_CHUNK, HGRN_CHUNK), HGRN_CHUNK)
            for j, cols in enumerate(head_cols):
                y = _rms(outs[j] + ofwd_ref[src, cols], ng_ref[...])
                g = g_ref[rows, cols]
                o_ref[rows, cols] = y * (g * _sigmoid(g))
            return carry

        lax.fori_loop(0, n_chunks, body, 0, unroll=2)


def _hgrn_tile_rows(T):
    return min(T, 2048)


def _hgrn2_mixer(z, lb_f, lb_b, norm_g):
    B, T, _ = z.shape
    tile_rows = _hgrn_tile_rows(T)
    n_tiles = T // tile_rows
    n_groups = A_HEADS // HGRN_HPS
    lb = jnp.stack([lb_f.reshape(n_groups, HGRN_COLS), lb_b.reshape(n_groups, HGRN_COLS)], axis=1)
    last = n_tiles - 1
    dmat, amask = _hgrn_tables()

    def both(s):
        return jnp.where(s < n_tiles, s, 2 * n_tiles - 1 - s)

    def fwd_only(s):
        return jnp.minimum(s, last)

    def bwd_only(s):
        return jnp.where(s < n_tiles, last, 2 * n_tiles - 1 - s)

    def zspec(part, tile_of):
        return pl.BlockSpec((None, tile_rows, HGRN_COLS),
                            lambda b, h, s: (b, tile_of(s), part * n_groups + h))

    return pl.pallas_call(
        functools.partial(_hgrn_kernel, n_tiles=n_tiles, tile_rows=tile_rows),
        out_shape=jax.ShapeDtypeStruct((B, T, A_WIDTH), jnp.float32),
        grid=(B, n_groups, 2 * n_tiles),
        in_specs=[
            zspec(0, both), zspec(1, fwd_only), zspec(2, bwd_only), zspec(3, both), zspec(4, bwd_only),
            pl.BlockSpec((None, 2, HGRN_COLS), lambda b, h, s: (h, 0, 0)),
            pl.BlockSpec((1, HEAD_DIM), lambda b, h, s: (0, 0)),
            pl.BlockSpec((None,) + dmat.shape[1:], lambda b, h, s: (s // n_tiles, 0, 0)),
            pl.BlockSpec((None,) + amask.shape[1:], lambda b, h, s: (s // n_tiles, 0, 0, 0)),
        ],
        out_specs=pl.BlockSpec((None, tile_rows, HGRN_COLS), lambda b, h, s: (b, bwd_only(s), h)),
        scratch_shapes=[pltpu.VMEM((T, HGRN_COLS), jnp.float32),
                        pltpu.VMEM((HGRN_HPS, HEAD_DIM, HEAD_DIM), jnp.float32),
                        pltpu.VMEM((tile_rows, HGRN_COLS), jnp.float32),
                        pltpu.VMEM((tile_rows, HGRN_COLS), jnp.bfloat16),
                        pltpu.VMEM((tile_rows, HGRN_COLS), jnp.bfloat16),
                        pltpu.VMEM((tile_rows // HGRN_CHUNK * 8, HGRN_COLS), jnp.float32)],
        compiler_params=pltpu.CompilerParams(
            dimension_semantics=("arbitrary", "arbitrary", "arbitrary"),
            vmem_limit_bytes=VMEM_LIMIT_BYTES,
        ),
        name="hgrn2",
    )(z, z, z, z, z, lb.astype(jnp.float32), norm_g.reshape(1, HEAD_DIM).astype(jnp.float32),
      jnp.asarray(dmat, jnp.bfloat16), jnp.asarray(amask, jnp.float32))


NA_KEYS = NA_WIN_R * GRID_W


def _na_bias_table(rpb):
    qc = np.arange(GRID_W)[:, None]
    kc = np.arange(GRID_W)[None, :]
    ws = np.clip(qc - NA_WIN_C // 2, 0, GRID_W - NA_WIN_C)
    col_ok = (kc >= ws) & (kc < ws + NA_WIN_C)
    dc = np.clip(kc - qc + NA_WIN_C - 1, 0, 2 * NA_WIN_C - 2)
    toe = jnp.where(jnp.asarray(col_ok)[None, None], rpb.astype(jnp.float32)[:, :, dc], NEG_INF)
    cases = []
    for c in range(NA_WIN_R):
        cases.append(jnp.concatenate([toe[:, c + j] for j in range(NA_WIN_R)], axis=-1))
    return jnp.stack(cases, axis=1)


def _na_kernel(q_ref, k_ref, v_ref, bias_ref, qg_ref, kg_ref, o_ref, kn_ref, *, rows, tile_grid_rows):
    qt = pl.program_id(2)

    @pl.when(qt == 0)
    def _():
        def kbody(i, carry):
            blk = pl.ds(pl.multiple_of(i * NA_KEYS, NA_KEYS), NA_KEYS)
            kn_ref[blk, :] = _bf16(_rms(k_ref[blk, :], kg_ref[...]))
            return carry

        lax.fori_loop(0, rows * GRID_W // NA_KEYS, kbody, 0)

    def body(rl, carry):
        r = qt * tile_grid_rows + rl
        rs = jnp.clip(r - NA_WIN_R // 2, 0, rows - NA_WIN_R)
        q0 = pl.multiple_of(rl * GRID_W, GRID_W)
        k0 = pl.multiple_of(rs * GRID_W, GRID_W)
        q = _rms(q_ref[pl.ds(q0, GRID_W), :], qg_ref[...]) * (1.0 / np.sqrt(HEAD_DIM))
        s = _dot_nt(q, kn_ref[pl.ds(k0, NA_KEYS), :]) + bias_ref[rs - r + NA_WIN_R - 1]
        m = jnp.max(s, axis=-1, keepdims=True)
        p = jnp.exp(s - m)
        den = jnp.sum(p, axis=-1, keepdims=True)
        o_ref[pl.ds(q0, GRID_W), :] = _dot(p, v_ref[pl.ds(k0, NA_KEYS), :]) / den
        return carry

    lax.fori_loop(0, tile_grid_rows, body, 0, unroll=4)


def _neighborhood_mixer(z, rpb, qnorm_g, knorm_g):
    B, T, _ = z.shape
    rows = T // GRID_W
    assert rows >= NA_WIN_R
    tile_grid_rows = min(rows, 32)
    tq = tile_grid_rows * GRID_W
    bias = _na_bias_table(rpb)
    gq = qnorm_g.reshape(1, HEAD_DIM).astype(jnp.float32)
    gk = knorm_g.reshape(1, HEAD_DIM).astype(jnp.float32)
    return pl.pallas_call(
        functools.partial(_na_kernel, rows=rows, tile_grid_rows=tile_grid_rows),
        out_shape=jax.ShapeDtypeStruct((B, T, B_WIDTH), jnp.float32),
        grid=(B, B_HEADS, T // tq),
        in_specs=[
            pl.BlockSpec((None, tq, HEAD_DIM), lambda b, h, t: (b, t, B_COL0 + h)),
            pl.BlockSpec((None, T, HEAD_DIM), lambda b, h, t: (b, 0, B_COL0 + B_HEADS + h)),
            pl.BlockSpec((None, T, HEAD_DIM), lambda b, h, t: (b, 0, B_COL0 + 2 * B_HEADS + h)),
            pl.BlockSpec((None, NA_WIN_R, GRID_W, NA_KEYS), lambda b, h, t: (h, 0, 0, 0)),
            pl.BlockSpec((1, HEAD_DIM), lambda b, h, t: (0, 0)),
            pl.BlockSpec((1, HEAD_DIM), lambda b, h, t: (0, 0)),
        ],
        out_specs=pl.BlockSpec((None, tq, HEAD_DIM), lambda b, h, t: (b, t, h)),
        scratch_shapes=[pltpu.VMEM((T, HEAD_DIM), jnp.bfloat16)],
        compiler_params=pltpu.CompilerParams(
            dimension_semantics=("arbitrary", "arbitrary", "arbitrary"),
            vmem_limit_bytes=VMEM_LIMIT_BYTES,
        ),
        name="neighborhood_attn",
    )(z, z, z, bias, gq, gk)


DIL_HALF = 64
DIL_QB = 128
assert all(w // (2 * d) == DIL_HALF for w, d in DIL_GROUPS)


def _rope_tables(T):
    half = HEAD_DIM // 2
    inv = 1.0 / (ROPE_THETA ** (jnp.arange(half, dtype=jnp.float32) * 2.0 / HEAD_DIM))
    ang = jnp.arange(T, dtype=jnp.float32)[:, None] * inv[None, :]
    cos, sin = jnp.cos(ang), jnp.sin(ang)
    return jnp.concatenate([cos, cos], axis=-1), jnp.concatenate([-sin, sin], axis=-1)


def _dil_kernel(q_ref, k_ref, v_ref, qcos_ref, qsin_ref, kcos_ref, ksin_ref, qg_ref, kg_ref,
                o_ref, lse_ref, kn_ref, *, L, Lq, KW, QB, KB):
    qi = pl.program_id(3)

    def norm_rope(x, g, cos, sin):
        xn = _rms(x, g)
        return xn * cos + pltpu.roll(xn, HEAD_DIM // 2, 1) * sin

    @pl.when(qi == 0)
    def _():
        def kbody(i, carry):
            r = pl.ds(pl.multiple_of(i * KB, KB), KB)
            kn_ref[r, :] = _bf16(norm_rope(k_ref[r, :], kg_ref[...], kcos_ref[r, :], ksin_ref[r, :]))
            return carry

        lax.fori_loop(0, L // KB, kbody, 0)

    def body(i, carry):
        q0l = pl.multiple_of(i * QB, QB)
        q0 = qi * Lq + q0l
        kb = pl.multiple_of(jnp.clip(q0 - DIL_HALF, 0, L - KW), DIL_HALF)
        rows = pl.ds(q0l, QB)
        q = norm_rope(q_ref[rows, :], qg_ref[...], qcos_ref[rows, :], qsin_ref[rows, :])
        s = _dot_nt(q * (1.0 / np.sqrt(HEAD_DIM)), kn_ref[pl.ds(kb, KW), :])
        qpos = q0 + lax.broadcasted_iota(jnp.int32, (QB, KW), 0)
        kpos = kb + lax.broadcasted_iota(jnp.int32, (QB, KW), 1)
        s = jnp.where(jnp.abs(qpos - kpos) <= DIL_HALF, s, NEG_INF)
        m = jnp.max(s, axis=-1, keepdims=True)
        p = jnp.exp(s - m)
        den = jnp.sum(p, axis=-1, keepdims=True)
        o_ref[rows, :] = _dot(p, v_ref[pl.ds(kb, KW), :]) / den
        lse_ref[rows, :] = jnp.broadcast_to(m + jnp.log(den), (QB, HEAD_DIM))
        return carry

    lax.fori_loop(0, Lq // QB, body, 0, unroll=2 if Lq // QB >= 2 else 1)


C_GROUP_COLS = 3 * C_HEADS_PER_GROUP * HEAD_DIM


def _c_proj_kernel(x_ref, g_ref, w_ref, o0_ref, o1_ref, o2_ref, xn_ref, res_ref, *, tm):
    gi = pl.program_id(1)

    @pl.when(gi == 0)
    def _():
        xn_ref[...] = _bf16(_rms(x_ref[...], g_ref[...]))

    res = jnp.dot(xn_ref[...], w_ref[...], preferred_element_type=jnp.float32)
    for g, ((_, d), o_ref) in enumerate(zip(DIL_GROUPS, (o0_ref, o1_ref, o2_ref))):
        @pl.when(gi == g)
        def _(d=d, o_ref=o_ref):
            if d == 1:
                o_ref[...] = res
            else:
                for c in range(C_GROUP_COLS // LANES):
                    res_ref[c] = res[:, c * LANES:(c + 1) * LANES]
                for j in range(d):
                    for c in range(C_GROUP_COLS // LANES):
                        c0 = j * C_GROUP_COLS + c * LANES
                        o_ref[:, c0:c0 + LANES] = res_ref[c, pl.ds(j, tm // d, stride=d), :]


def _c_proj(hf, gain, w_c, *, tm=512):
    n, k = hf.shape
    hp = C_HEADS_PER_GROUP
    cols = []
    for g in range(len(DIL_GROUPS)):
        for part in range(3):
            c0 = part * C_WIDTH + g * hp * HEAD_DIM
            cols.append(w_c[:, c0:c0 + hp * HEAD_DIM])
    w_perm = jnp.concatenate(cols, axis=1).astype(jnp.bfloat16)
    return pl.pallas_call(
        functools.partial(_c_proj_kernel, tm=tm),
        out_shape=tuple(jax.ShapeDtypeStruct((n // d, d * C_GROUP_COLS), jnp.float32) for _, d in DIL_GROUPS),
        grid=(n // tm, len(DIL_GROUPS)),
        in_specs=[pl.BlockSpec((tm, k), lambda i, g: (i, 0)),
                  pl.BlockSpec((1, k), lambda i, g: (0, 0)),
                  pl.BlockSpec((k, C_GROUP_COLS), lambda i, g: (0, g))],
        out_specs=tuple(pl.BlockSpec((tm // d, d * C_GROUP_COLS), lambda i, g: (i, 0)) for _, d in DIL_GROUPS),
        scratch_shapes=[pltpu.VMEM((tm, k), jnp.bfloat16),
                        pltpu.VMEM((C_GROUP_COLS // LANES, tm, LANES), jnp.float32)],
        compiler_params=pltpu.CompilerParams(dimension_semantics=("arbitrary", "arbitrary"),
                                             vmem_limit_bytes=VMEM_LIMIT_BYTES),
        name="c_proj",
    )(hf, gain.reshape(1, k).astype(jnp.float32), w_perm)


def _dilated_group(zg, B, T, d, qnorm_g, knorm_g, cos2, sin2):
    hp = C_HEADS_PER_GROUP
    L = T // d
    Lq = min(L, 1024)
    QB = min(DIL_QB, Lq)
    KW = min(2 * DIL_QB, L)
    KB = min(L, 512)
    zv = zg.reshape(B, L, d * C_GROUP_COLS)
    cosv = cos2.reshape(L, d * HEAD_DIM)
    sinv = sin2.reshape(L, d * HEAD_DIM)
    gq = qnorm_g.reshape(1, HEAD_DIM).astype(jnp.float32)
    gk = knorm_g.reshape(1, HEAD_DIM).astype(jnp.float32)

    def zcol(part):
        return lambda b, j, h, qi: (j * 3 + part) * hp + h

    out_sds = jax.ShapeDtypeStruct((B, L, d * hp * HEAD_DIM), jnp.float32)
    out_spec = pl.BlockSpec((None, Lq, HEAD_DIM), lambda b, j, h, qi: (b, qi, j * hp + h))
    o, lse = pl.pallas_call(
        functools.partial(_dil_kernel, L=L, Lq=Lq, KW=KW, QB=QB, KB=KB),
        out_shape=(out_sds, out_sds),
        grid=(B, d, hp, L // Lq),
        in_specs=[
            pl.BlockSpec((None, Lq, HEAD_DIM), lambda b, j, h, qi: (b, qi, zcol(0)(b, j, h, qi))),
            pl.BlockSpec((None, L, HEAD_DIM), lambda b, j, h, qi: (b, 0, zcol(1)(b, j, h, qi))),
            pl.BlockSpec((None, L, HEAD_DIM), lambda b, j, h, qi: (b, 0, zcol(2)(b, j, h, qi))),
            pl.BlockSpec((Lq, HEAD_DIM), lambda b, j, h, qi: (qi, j)),
            pl.BlockSpec((Lq, HEAD_DIM), lambda b, j, h, qi: (qi, j)),
            pl.BlockSpec((L, HEAD_DIM), lambda b, j, h, qi: (0, j)),
            pl.BlockSpec((L, HEAD_DIM), lambda b, j, h, qi: (0, j)),
            pl.BlockSpec((1, HEAD_DIM), lambda b, j, h, qi: (0, 0)),
            pl.BlockSpec((1, HEAD_DIM), lambda b, j, h, qi: (0, 0)),
        ],
        out_specs=(out_spec, out_spec),
        scratch_shapes=[pltpu.VMEM((L, HEAD_DIM), jnp.bfloat16)],
        compiler_params=pltpu.CompilerParams(
            dimension_semantics=("arbitrary", "arbitrary", "arbitrary", "arbitrary"),
            vmem_limit_bytes=VMEM_LIMIT_BYTES,
        ),
        name=f"dilated_attn_d{d}",
    )(zv, zv, zv, cosv, sinv, cosv, sinv, gq, gk)
    return o.reshape(B * L, d * hp * HEAD_DIM), lse.reshape(B * L, d * hp * HEAD_DIM)


def _dilated_mixer(zgs, B, T, qnorm_g, knorm_g):
    cos2, sin2 = _rope_tables(T)
    outs, lses = [], []
    for zg, (_, d) in zip(zgs, DIL_GROUPS):
        o, lse = _dilated_group(zg, B, T, d, qnorm_g, knorm_g, cos2, sin2)
        outs.append(o)
        lses.append(lse)
    return outs, lses


def _out_proj_kernel(h_ref, a_ref, b_ref, o0_ref, o1_ref, o2_ref, l0_ref, l1_ref, l2_ref, w_ref,
                     g2_ref, wr_ref, out_ref, xn_ref, aff_ref, mix_ref, stage_ref, hrow_ref, *, tm, tn):
    gw = C_HEADS_PER_GROUP * HEAD_DIM
    n_col = D_MODEL // tn

    def token_major(src_ref, d, slot):
        if d == 1:
            return src_ref[...]
        nc = gw // LANES
        for j in range(d):
            for c in range(nc):
                c0 = j * gw + c * LANES
                stage_ref[slot * nc + c, pl.ds(j, tm // d, stride=d), :] = src_ref[:, c0:c0 + LANES]
        return jnp.concatenate([stage_ref[slot * nc + c] for c in range(nc)], axis=1)

    @pl.when(pl.program_id(1) == 0)
    def _():
        mix_ref[:, :A_WIDTH] = _bf16(a_ref[...])
        mix_ref[:, A_WIDTH:A_WIDTH + B_WIDTH] = _bf16(b_ref[...])
        dils = [d for _, d in DIL_GROUPS]
        lses = [token_major(l_ref, d, g) for g, (l_ref, d) in enumerate(zip((l0_ref, l1_ref, l2_ref), dils))]
        m = jnp.maximum(jnp.maximum(lses[0], lses[1]), lses[2])
        es = [jnp.exp(l - m) for l in lses]
        inv = 1.0 / (es[0] + es[1] + es[2])
        for g, (o_ref, d) in enumerate(zip((o0_ref, o1_ref, o2_ref), dils)):
            c0 = A_WIDTH + B_WIDTH + g * gw
            mix_ref[:, c0:c0 + gw] = _bf16(token_major(o_ref, d, len(dils) + g) * (es[g] * inv))

    j = pl.program_id(1)
    h_new = h_ref[...] + jnp.dot(mix_ref[...], w_ref[...], preferred_element_type=jnp.float32)
    out_ref[...] = h_new
    for jj in range(n_col):
        @pl.when(j == jj)
        def _(jj=jj):
            hrow_ref[:, jj * tn:(jj + 1) * tn] = h_new

    @pl.when(j == n_col - 1)
    def _():
        xn = _rms(hrow_ref[...], g2_ref[...])
        xn_ref[...] = _bf16(xn)
        logits = jnp.dot(xn, wr_ref[...], preferred_element_type=jnp.float32, precision=lax.Precision.HIGHEST)
        lane = lax.broadcasted_iota(jnp.int32, logits.shape, 1)
        logits = jnp.where(lane < N_EXPERTS, logits, NEG_INF)
        e = jnp.exp(logits - jnp.max(logits, axis=-1, keepdims=True))
        aff_ref[...] = e / jnp.sum(e, axis=-1, keepdims=True)


def _out_proj_router(hf, a_out, b_out, c_outs, c_lses, w_out, norm2_g, w_router, *, tm=512, tn=512):
    n = hf.shape[0]
    gw = C_HEADS_PER_GROUP * HEAD_DIM
    row = lambda width: pl.BlockSpec((tm, width), lambda i, j: (i, 0))
    dil = [pl.BlockSpec((tm // d, d * gw), lambda i, j: (i, 0)) for _, d in DIL_GROUPS]
    wr = jnp.zeros((D_MODEL, LANES), jnp.float32).at[:, :N_EXPERTS].set(w_router.astype(jnp.float32))
    return pl.pallas_call(
        functools.partial(_out_proj_kernel, tm=tm, tn=tn),
        out_shape=(jax.ShapeDtypeStruct((n, D_MODEL), jnp.float32),
                   jax.ShapeDtypeStruct((n, D_MODEL), jnp.bfloat16),
                   jax.ShapeDtypeStruct((n, LANES), jnp.float32)),
        grid=(n // tm, D_MODEL // tn),
        in_specs=[pl.BlockSpec((tm, tn), lambda i, j: (i, j)), row(A_WIDTH), row(B_WIDTH)]
        + dil + dil + [pl.BlockSpec((D_MODEL, tn), lambda i, j: (0, j)),
                       pl.BlockSpec((1, D_MODEL), lambda i, j: (0, 0)),
                       pl.BlockSpec((D_MODEL, LANES), lambda i, j: (0, 0))],
        out_specs=(pl.BlockSpec((tm, tn), lambda i, j: (i, j)), row(D_MODEL), row(LANES)),
        scratch_shapes=[pltpu.VMEM((tm, D_MODEL), jnp.bfloat16),
                        pltpu.VMEM((2 * len(DIL_GROUPS) * gw // LANES, tm, LANES), jnp.float32),
                        pltpu.VMEM((tm, D_MODEL), jnp.float32)],
        compiler_params=pltpu.CompilerParams(
            dimension_semantics=("arbitrary", "arbitrary"),
            vmem_limit_bytes=VMEM_LIMIT_BYTES,
        ),
        name="out_proj_router",
    )(hf, a_out.reshape(n, A_WIDTH), b_out.reshape(n, B_WIDTH), *c_outs, *c_lses,
      w_out.astype(jnp.bfloat16), norm2_g.reshape(1, D_MODEL).astype(jnp.float32), wr)


LANES = 128
FFN_ROWS = 1024
HALF_D = D_MODEL // 2
COMBINE_ROWS = 512


def _pack_bf16_pairs(lo, hi):
    bits = lambda v: lax.bitcast_convert_type(_bf16(v).astype(jnp.float32), jnp.uint32)
    return (bits(lo) >> 16) | (bits(hi) & jnp.uint32(0xFFFF0000))


def _unpack_bf16_pairs(words):
    return (lax.bitcast_convert_type(words << 16, jnp.float32),
            lax.bitcast_convert_type(words & jnp.uint32(0xFFFF0000), jnp.float32))


def _expert_kernel(dest_hbm, x_ref, g_ref, wg_ref, wu_ref, wd_ref, y_hbm, ybuf, dbuf, ysem, dsem,
                   *, n_steps, steps_per_expert):
    step = pl.program_id(0) * steps_per_expert + pl.program_id(1)
    slot = step % 2

    def dest_copy(s, sl):
        return pltpu.make_async_copy(dest_hbm.at[s], dbuf.at[sl], dsem.at[sl])

    def scatter_wait(sl):
        pltpu.make_async_copy(ybuf.at[sl], ybuf.at[sl], ysem.at[sl]).wait()

    @pl.when(step == 0)
    def _():
        dest_copy(0, 0).start()

    @pl.when(step + 1 < n_steps)
    def _():
        dest_copy(step + 1, 1 - slot).start()

    @pl.when(step >= 2)
    def _():
        scatter_wait(slot)

    x = x_ref[...]
    gate = jnp.dot(x, wg_ref[...], preferred_element_type=jnp.float32)
    hdn = gate * _sigmoid(gate) * jnp.dot(x, wu_ref[...], preferred_element_type=jnp.float32)
    y = jnp.dot(_bf16(hdn), wd_ref[...], preferred_element_type=jnp.float32)
    y = y * jnp.concatenate([g_ref[...]] * (D_MODEL // LANES), axis=1)
    ybuf[slot] = _pack_bf16_pairs(y[:, :HALF_D], y[:, HALF_D:])

    dest_copy(step, slot).wait()
    for r in range(FFN_ROWS):
        pltpu.make_async_copy(ybuf.at[slot, pl.ds(r, 1), :],
                              y_hbm.at[pl.ds(dbuf[slot, r], 1), :], ysem.at[slot]).start()

    @pl.when(step == n_steps - 1)
    def _():
        scatter_wait(slot)
        if n_steps >= 2:
            scatter_wait(1 - slot)


def _expert_ffn(xe, gates_rep, dest, w_gate, w_up, w_down, n_tokens):
    rows, d = xe.shape
    cap = rows // N_EXPERTS
    f = w_gate.shape[-1]
    spe = cap // FFN_ROWS
    n_steps = N_EXPERTS * spe
    return pl.pallas_call(
        functools.partial(_expert_kernel, n_steps=n_steps, steps_per_expert=spe),
        out_shape=jax.ShapeDtypeStruct((N_EXPERTS * n_tokens, HALF_D), jnp.uint32),
        grid=(N_EXPERTS, spe),
        in_specs=[
            pl.BlockSpec(memory_space=pl.ANY),
            pl.BlockSpec((FFN_ROWS, d), lambda e, t: (e * spe + t, 0)),
            pl.BlockSpec((FFN_ROWS, LANES), lambda e, t: (e * spe + t, 0)),
            pl.BlockSpec((None, d, f), lambda e, t: (e, 0, 0), pipeline_mode=pl.Buffered(1)),
            pl.BlockSpec((None, d, f), lambda e, t: (e, 0, 0), pipeline_mode=pl.Buffered(1)),
            pl.BlockSpec((None, f, d), lambda e, t: (e, 0, 0), pipeline_mode=pl.Buffered(1)),
        ],
        out_specs=pl.BlockSpec(memory_space=pl.ANY),
        scratch_shapes=[pltpu.VMEM((2, FFN_ROWS, HALF_D), jnp.uint32),
                        pltpu.SMEM((2, FFN_ROWS), jnp.int32),
                        pltpu.SemaphoreType.DMA((2,)),
                        pltpu.SemaphoreType.DMA((2,))],
        compiler_params=pltpu.CompilerParams(dimension_semantics=("arbitrary", "arbitrary"),
                                             vmem_limit_bytes=VMEM_LIMIT_BYTES),
        name="expert_ffn",
    )(dest.reshape(n_steps, FFN_ROWS), xe, gates_rep, w_gate.astype(jnp.bfloat16),
      w_up.astype(jnp.bfloat16), w_down.astype(jnp.bfloat16))


COMBINE_RANKS = 4
COMBINE_STEPS = N_EXPERTS // COMBINE_RANKS
PLE_TN = 512
PLE_STEPS = D_MODEL // PLE_TN


def _combine_ple_kernel(tmax_ref, h_ref, cnt_ref, *refs):
    y_refs = refs[:COMBINE_RANKS]
    (p_ref, gg_ref, pg_ref, wp_ref, wg_ref, out_ref, h2_ref, xn_ref, e_ref) = refs[COMBINE_RANKS:]
    i, step = pl.program_id(0), pl.program_id(1)
    cols_per_half = HALF_D // PLE_TN

    @pl.when(step == 0)
    def _():
        for jj in range(PLE_STEPS):
            h2_ref[jj] = h_ref[:, jj * PLE_TN:(jj + 1) * PLE_TN]

    for q, y_ref in enumerate(y_refs):
        @pl.when((step < COMBINE_STEPS) & (step * COMBINE_RANKS + q < tmax_ref[i]))
        def _(q=q, y_ref=y_ref):
            chosen = jnp.concatenate([cnt_ref[...]] * (HALF_D // LANES), axis=1) > step * COMBINE_RANKS + q
            halves = _unpack_bf16_pairs(y_ref[...])
            for hh, half in enumerate(halves):
                add = jnp.where(chosen, half, 0.0)
                for c in range(cols_per_half):
                    h2_ref[hh * cols_per_half + c] += add[:, c * PLE_TN:(c + 1) * PLE_TN]

    @pl.when(step == COMBINE_STEPS)
    def _():
        h2 = jnp.concatenate([h2_ref[jj] for jj in range(PLE_STEPS)], axis=1)
        xn_ref[...] = _bf16(_rms(h2, gg_ref[...]))
        e = _rms(jnp.dot(_bf16(p_ref[...]), wp_ref[...], preferred_element_type=jnp.float32), pg_ref[...])
        for jj in range(PLE_STEPS):
            e_ref[jj] = e[:, jj * PLE_TN:(jj + 1) * PLE_TN]

    @pl.when(step >= COMBINE_STEPS)
    def _():
        j = step - COMBINE_STEPS
        gate = _sigmoid(jnp.dot(xn_ref[...], wg_ref[...], preferred_element_type=jnp.float32))
        out_ref[...] = h2_ref[j] + gate * e_ref[j]


def _combine_ple(hf, slabs, cnt_rep, tile_max, p, w_ple, ple_norm_g, gate_norm_g, w_ple_gate):
    n, d = hf.shape
    tm = COMBINE_ROWS
    pd = p.shape[1]
    slabs = slabs.reshape(N_EXPERTS, n, HALF_D)
    vec = lambda g: g.reshape(1, D_MODEL).astype(jnp.float32)

    def slab_spec(q):
        return pl.BlockSpec((None, tm, HALF_D), lambda i, s, tmax: (
            jnp.minimum(jnp.minimum(s, COMBINE_STEPS - 1) * COMBINE_RANKS + q, jnp.maximum(tmax[i], 1) - 1), i, 0))

    col = lambda i, s, tmax: jnp.maximum(s - COMBINE_STEPS, 0)
    return pl.pallas_call(
        _combine_ple_kernel,
        out_shape=jax.ShapeDtypeStruct((n, d), jnp.float32),
        grid_spec=pltpu.PrefetchScalarGridSpec(
            num_scalar_prefetch=1,
            grid=(n // tm, COMBINE_STEPS + PLE_STEPS),
            in_specs=[pl.BlockSpec((tm, d), lambda i, s, tmax: (i, 0)),
                      pl.BlockSpec((tm, LANES), lambda i, s, tmax: (i, 0))]
            + [slab_spec(q) for q in range(COMBINE_RANKS)]
            + [pl.BlockSpec((tm, pd), lambda i, s, tmax: (i, 0)),
               pl.BlockSpec((1, d), lambda i, s, tmax: (0, 0)),
               pl.BlockSpec((1, d), lambda i, s, tmax: (0, 0)),
               pl.BlockSpec((pd, d), lambda i, s, tmax: (0, 0)),
               pl.BlockSpec((d, PLE_TN), lambda i, s, tmax: (0, col(i, s, tmax)))],
            out_specs=pl.BlockSpec((tm, PLE_TN), lambda i, s, tmax: (i, col(i, s, tmax))),
            scratch_shapes=[pltpu.VMEM((PLE_STEPS, tm, PLE_TN), jnp.float32),
                            pltpu.VMEM((tm, d), jnp.bfloat16),
                            pltpu.VMEM((PLE_STEPS, tm, PLE_TN), jnp.float32)],
        ),
        compiler_params=pltpu.CompilerParams(dimension_semantics=("arbitrary", "arbitrary"),
                                             vmem_limit_bytes=VMEM_LIMIT_BYTES),
        name="ffn_combine_ple",
    )(tile_max, hf, cnt_rep, *([slabs] * COMBINE_RANKS), p, vec(gate_norm_g), vec(ple_norm_g),
      w_ple.astype(jnp.bfloat16), w_ple_gate.astype(jnp.bfloat16))


def _expert_choice_slabs(xn, aff, w_gate, w_up, w_down):
    n = xn.shape[0]
    cap = (EC_CAPACITY_FACTOR * n) // N_EXPERTS
    gates, idx = lax.top_k(aff[:, :N_EXPERTS].T, cap)
    xe = xn[idx.reshape(-1)]
    chosen = jnp.zeros((N_EXPERTS, n), jnp.int32).at[jnp.arange(N_EXPERTS)[:, None], idx].set(1)
    rank = jnp.cumsum(chosen, axis=0) - chosen
    dest = jnp.take_along_axis(rank, idx, axis=1) * n + idx
    cnt = jnp.sum(chosen, axis=0)
    slabs = _expert_ffn(xe, jnp.broadcast_to(gates.reshape(-1, 1), (N_EXPERTS * cap, LANES)),
                        dest.reshape(-1).astype(jnp.int32), w_gate, w_up, w_down, n)
    cnt_rep = jnp.broadcast_to(cnt[:, None], (n, LANES)).astype(jnp.int32)
    tile_max = jnp.max(cnt.reshape(n // COMBINE_ROWS, COMBINE_ROWS), axis=1).astype(jnp.int32)
    return slabs, cnt_rep, tile_max


def _layer(h, p_i, lb_f, lb_b, norm1_g, w_in, a_norm_g, b_qnorm_g, b_knorm_g, b_rpb,
           c_qnorm_g, c_knorm_g, w_out, norm2_g, w_router, w_gate, w_up, w_down,
           w_ple, ple_norm_g, gate_norm_g, w_ple_gate):
    B, T, _ = h.shape
    N = B * T
    hf = h.reshape(N, D_MODEL)
    ab_width = IN_WIDTH - 3 * C_WIDTH
    z_ab = _matmul(hf, w_in[:, :ab_width], norm1_g, tn=768).reshape(B, T, ab_width)
    z_c = _c_proj(hf, norm1_g, w_in[:, ab_width:])
    a_out = _hgrn2_mixer(z_ab, lb_f, lb_b, a_norm_g)
    b_out = _neighborhood_mixer(z_ab, b_rpb, b_qnorm_g, b_knorm_g)
    c_outs, c_lses = _dilated_mixer(z_c, B, T, c_qnorm_g, c_knorm_g)
    hf, xn, aff = _out_proj_router(hf, a_out, b_out, c_outs, c_lses, w_out, norm2_g, w_router)
    slabs, cnt_rep, tile_max = _expert_choice_slabs(xn, aff, w_gate, w_up, w_down)
    out = _combine_ple(hf, slabs, cnt_rep, tile_max, p_i.reshape(N, -1), w_ple, ple_norm_g, gate_norm_g,
                       w_ple_gate)
    return out.reshape(B, T, D_MODEL)


def _trunk(h, p, lb, weights):
    (norm1_g, w_in, a_norm_g, b_qnorm_g, b_knorm_g, b_rpb, c_qnorm_g, c_knorm_g, w_out,
     norm2_g, w_router, w_gate, w_up, w_down, w_ple, ple_norm_g, gate_norm_g, w_ple_gate) = weights
    for i in range(DEPTH):
        h = _layer(h, p[i], lb[0, i], lb[1, i], norm1_g[i], w_in[i], a_norm_g[i], b_qnorm_g[i],
                   b_knorm_g[i], b_rpb[i], c_qnorm_g[i], c_knorm_g[i], w_out[i], norm2_g[i],
                   w_router[i], w_gate[i], w_up[i], w_down[i], w_ple[i], ple_norm_g[i],
                   gate_norm_g[i], w_ple_gate[i])
    return h


def kernel(x_prompt, x_sample, p_prompt, p_sample, norm1_g, w_in, lb_logits, a_norm_g,
           b_qnorm_g, b_knorm_g, b_rpb, c_qnorm_g, c_knorm_g, w_out, norm2_g, w_router,
           w_gate, w_up, w_down, w_ple, ple_norm_g, gate_norm_g, w_ple_gate):
    pr = jax.nn.softmax(lb_logits.astype(jnp.float32), axis=1)
    lb = jnp.cumsum(pr, axis=1) - pr[:, :1]
    weights = (norm1_g, w_in, a_norm_g, b_qnorm_g, b_knorm_g, b_rpb, c_qnorm_g, c_knorm_g, w_out,
               norm2_g, w_router, w_gate, w_up, w_down, w_ple, ple_norm_g, gate_norm_g, w_ple_gate)
    y_prompt = _trunk(x_prompt, p_prompt, lb, weights)
    y_sample = _trunk(x_sample, p_sample, lb, weights)
    return (y_prompt, y_sample)
```

```python
import functools

import jax
import jax.numpy as jnp
import numpy as np
from jax import lax
from jax.experimental import pallas as pl
from jax.experimental.pallas import tpu as pltpu

D_MODEL = 2048
DEPTH = 4
HEAD_DIM = 128
A_HEADS = 6
B_HEADS = 4
C_HEADS = 6
A_WIDTH = A_HEADS * HEAD_DIM
B_WIDTH = B_HEADS * HEAD_DIM
C_WIDTH = C_HEADS * HEAD_DIM
IN_WIDTH = 5 * A_WIDTH + 3 * B_WIDTH + 3 * C_WIDTH
HGRN_CHUNK = 128
GRID_W = 64
NA_WIN_R = 8
NA_WIN_C = 16
DIL_GROUPS = ((128, 1), (512, 4), (2048, 16))
C_HEADS_PER_GROUP = C_HEADS // len(DIL_GROUPS)
ROPE_THETA = 10000.0
N_EXPERTS = 16
EC_CAPACITY_FACTOR = 2
EPS = 1e-6

VMEM_LIMIT_BYTES = 56 * 1024 * 1024
NEG_INF = -1e30
Z_COLS = IN_WIDTH // HEAD_DIM
B_COL0 = 5 * A_HEADS
C_COL0 = B_COL0 + 3 * B_HEADS


def _sigmoid(x):
    return 1.0 / (1.0 + jnp.exp(-x))


def _bf16(x):
    return x.astype(jnp.bfloat16)


def _dot(a, b):
    return jnp.dot(_bf16(a), _bf16(b), preferred_element_type=jnp.float32)


def _dot_nt(a, b):
    return lax.dot_general(_bf16(a), _bf16(b), (((1,), (1,)), ((), ())),
                           preferred_element_type=jnp.float32)


def _dot_tn(a, b):
    return lax.dot_general(_bf16(a), _bf16(b), (((0,), (0,)), ((), ())),
                           preferred_element_type=jnp.float32)


def _rms(x, g):
    return x * lax.rsqrt(jnp.mean(x * x, axis=-1, keepdims=True) + EPS) * g


def _norm_matmul_kernel(x_ref, g_ref, w_ref, o_ref, xn_ref, *, normalize):
    @pl.when(pl.program_id(1) == 0)
    def _():
        x = x_ref[...]
        if normalize:
            x = _rms(x, g_ref[...])
        xn_ref[...] = x.astype(jnp.bfloat16)

    o_ref[...] = jnp.dot(xn_ref[...], w_ref[...], preferred_element_type=jnp.float32)


def _matmul(x, w, gain=None, *, tm=1024, tn=512):
    n, k = x.shape
    m = w.shape[1]
    tm = min(tm, n)
    tn = min(tn, m)
    assert n % tm == 0 and m % tn == 0
    normalize = gain is not None
    g = (gain if normalize else jnp.ones((k,), jnp.float32)).reshape(1, k).astype(jnp.float32)
    return pl.pallas_call(
        functools.partial(_norm_matmul_kernel, normalize=normalize),
        out_shape=jax.ShapeDtypeStruct((n, m), jnp.float32),
        grid=(n // tm, m // tn),
        in_specs=[
            pl.BlockSpec((tm, k), lambda i, j: (i, 0)),
            pl.BlockSpec((1, k), lambda i, j: (0, 0)),
            pl.BlockSpec((k, tn), lambda i, j: (0, j)),
        ],
        out_specs=pl.BlockSpec((tm, tn), lambda i, j: (i, j)),
        scratch_shapes=[pltpu.VMEM((tm, k), jnp.bfloat16)],
        compiler_params=pltpu.CompilerParams(
            dimension_semantics=("arbitrary", "arbitrary"),
            vmem_limit_bytes=VMEM_LIMIT_BYTES,
        ),
        name="norm_matmul" if normalize else "matmul",
    )(x, g, w.astype(jnp.bfloat16))


HGRN_LEVELS = tuple(HGRN_CHUNK >> (i + 1) for i in range(HGRN_CHUNK.bit_length() - 1))
HGRN_NLEV = len(HGRN_LEVELS)
HGRN_HPS = 2
HGRN_COLS = HGRN_HPS * HEAD_DIM


def _hgrn_tables():
    C = HGRN_CHUNK
    t = np.arange(C)[:, None]
    u = np.arange(C)[None, :]
    dmats, amasks = [], []
    for reverse in (False, True):
        cum = ((u >= t) if reverse else (u <= t)).astype(np.float32)
        blocks, masks = [], []
        for m in HGRN_LEVELS:
            pos = np.arange(C) % (2 * m)
            ref = np.arange(C) - pos + (m if reverse else m - 1)
            blocks.append(cum - cum[ref])
            later = (pos < m) if reverse else (pos >= m)
            same = (t // (2 * m)) == (u // (2 * m))
            masks.append((same & later[:, None] & ~later[None, :]).astype(np.float32))
        last = 0 if reverse else C - 1
        blocks += [cum, cum[last:last + 1] - cum]
        masks.append(np.eye(C, dtype=np.float32))
        assert np.array_equal(sum(masks), cum)
        dmats.append(np.tile(np.concatenate(blocks, axis=0), (1, 2)))
        amasks.append(np.stack(masks, axis=0))
    return np.stack(dmats, axis=0), np.stack(amasks, axis=0)


def _hgrn_intra(q, x, v, lb, dmat, amask, reverse):
    C = HGRN_CHUNK
    f = lb + (1.0 - lb) * _sigmoid(x)
    lf = jnp.log(f)
    k = 1.0 - f
    hi = _bf16(lf)
    mid = _bf16(lf - hi.astype(jnp.float32))
    dall = jnp.dot(dmat[...], jnp.concatenate([hi, mid], axis=0),
                   preferred_element_type=jnp.float32)

    def stack(a):
        return jnp.concatenate([a[:, j * HEAD_DIM:(j + 1) * HEAD_DIM] for j in range(HGRN_HPS)], axis=0)

    def head_blocks(p):
        return [p[j * C:(j + 1) * C, j * C:(j + 1) * C] for j in range(HGRN_HPS)]

    qs, ks = stack(q), stack(k)
    row = lax.broadcasted_iota(jnp.int32, (HGRN_HPS * C, HEAD_DIM), 0)
    a = [amask[HGRN_NLEV] * p for p in head_blocks(_dot_nt(qs, ks))]
    for l, m in enumerate(HGRN_LEVELS):
        later = ((row & m) == 0) if reverse else ((row & m) != 0)
        xh = _bf16(jnp.where(later, qs, ks) * jnp.exp(-jnp.abs(stack(dall[l * C:(l + 1) * C]))))
        pairs = lax.dot_general(xh, xh, (((1,), (1,)), ((), ())), preferred_element_type=jnp.float32)
        a = [a_j + amask[l] * p for a_j, p in zip(a, head_blocks(pairs))]
    oi = jnp.concatenate([_dot(a[j], v[:, j * HEAD_DIM:(j + 1) * HEAD_DIM]) for j in range(HGRN_HPS)],
                         axis=1)
    b = dall[HGRN_NLEV * C:(HGRN_NLEV + 1) * C]
    b_rest = dall[(HGRN_NLEV + 1) * C:(HGRN_NLEV + 2) * C]
    b_last = b[0:1, :] if reverse else b[C - 1:C, :]
    return oi, q * jnp.exp(b), k * jnp.exp(b_rest), jnp.exp(b_last)


def _hgrn_kernel(q_ref, ff_ref, fb_ref, v_ref, g_ref, lb_ref, ng_ref, dmat_ref, amask_ref, o_ref,
                 ofwd_ref, st_ref, oi_ref, qt_ref, kt_ref, eb_ref, *, n_tiles, tile_rows):
    s = pl.program_id(2)
    n_chunks = tile_rows // HGRN_CHUNK
    head_cols = [slice(j * HEAD_DIM, (j + 1) * HEAD_DIM) for j in range(HGRN_HPS)]

    @pl.when((s == 0) | (s == n_tiles))
    def _():
        st_ref[...] = jnp.zeros_like(st_ref)

    def intra_pass(f_ref, direction, reverse):
        def body(c, carry):
            rows = pl.ds(pl.multiple_of(c * HGRN_CHUNK, HGRN_CHUNK), HGRN_CHUNK)
            qp = q_ref[rows, :]
            oi, qt, kt, eb = _hgrn_intra(qp * _sigmoid(qp), f_ref[rows, :], v_ref[rows, :],
                                         lb_ref[direction:direction + 1, :], dmat_ref, amask_ref, reverse)
            oi_ref[rows, :] = oi
            qt_ref[rows, :] = _bf16(qt)
            kt_ref[rows, :] = _bf16(kt)
            eb_ref[pl.ds(pl.multiple_of(c * 8, 8), 8), :] = jnp.broadcast_to(eb, (8, HGRN_COLS))
            return carry

        lax.fori_loop(0, n_chunks, body, 0, unroll=2)

    def state_step(c):
        rows = pl.ds(pl.multiple_of(c * HGRN_CHUNK, HGRN_CHUNK), HGRN_CHUNK)
        outs = []
        for j, cols in enumerate(head_cols):
            st = st_ref[j]
            outs.append(oi_ref[rows, cols] + _dot_nt(qt_ref[rows, cols], st))
            st_ref[j] = st * eb_ref[pl.ds(pl.multiple_of(c * 8, 8), 1), cols] + _dot_tn(v_ref[rows, cols],
                                                                                       kt_ref[rows, cols])
        return rows, outs

    @pl.when(s < n_tiles)
    def _():
        intra_pass(ff_ref, 0, False)
        base = s * tile_rows

        def body(c, carry):
            rows, outs = state_step(c)
            dst = pl.ds(pl.multiple_of(base + c * HGRN_CHUNK, HGRN_CHUNK), HGRN_CHUNK)
            for j, cols in enumerate(head_cols):
                ofwd_ref[dst, cols] = outs[j]
            return carry

        lax.fori_loop(0, n_chunks, body, 0, unroll=2)

    @pl.when(s >= n_tiles)
    def _():
        intra_pass(fb_ref, 1, True)
        base = (2 * n_tiles - 1 - s) * tile_rows

        def body(ci, carry):
            c = n_chunks - 1 - ci
            rows, outs = state_step(c)
            src = pl.ds(pl.multiple_of(base + c * HGRN_CHUNK, HGRN_CHUNK), HGRN_CHUNK)
            for j, cols in enumerate(head_cols):
                y = _rms(outs[j] + ofwd_ref[src, cols], ng_ref[...])
                g = g_ref[rows, cols]
                o_ref[rows, cols] = y * (g * _sigmoid(g))
            return carry

        lax.fori_loop(0, n_chunks, body, 0, unroll=2)


def _hgrn_tile_rows(T):
    return min(T, 2048)


def _hgrn2_mixer(z, lb_f, lb_b, norm_g):
    B, T, _ = z.shape
    tile_rows = _hgrn_tile_rows(T)
    n_tiles = T // tile_rows
    n_groups = A_HEADS // HGRN_HPS
    lb = jnp.stack([lb_f.reshape(n_groups, HGRN_COLS), lb_b.reshape(n_groups, HGRN_COLS)], axis=1)
    last = n_tiles - 1
    dmat, amask = _hgrn_tables()

    def both(s):
        return jnp.where(s < n_tiles, s, 2 * n_tiles - 1 - s)

    def fwd_only(s):
        return jnp.minimum(s, last)

    def bwd_only(s):
        return jnp.where(s < n_tiles, last, 2 * n_tiles - 1 - s)

    def zspec(part, tile_of):
        return pl.BlockSpec((None, tile_rows, HGRN_COLS),
                            lambda b, h, s: (b, tile_of(s), part * n_groups + h))

    return pl.pallas_call(
        functools.partial(_hgrn_kernel, n_tiles=n_tiles, tile_rows=tile_rows),
        out_shape=jax.ShapeDtypeStruct((B, T, A_WIDTH), jnp.float32),
        grid=(B, n_groups, 2 * n_tiles),
        in_specs=[
            zspec(0, both), zspec(1, fwd_only), zspec(2, bwd_only), zspec(3, both), zspec(4, bwd_only),
            pl.BlockSpec((None, 2, HGRN_COLS), lambda b, h, s: (h, 0, 0)),
            pl.BlockSpec((1, HEAD_DIM), lambda b, h, s: (0, 0)),
            pl.BlockSpec((None,) + dmat.shape[1:], lambda b, h, s: (s // n_tiles, 0, 0)),
            pl.BlockSpec((None,) + amask.shape[1:], lambda b, h, s: (s // n_tiles, 0, 0, 0)),
        ],
        out_specs=pl.BlockSpec((None, tile_rows, HGRN_COLS), lambda b, h, s: (b, bwd_only(s), h)),
        scratch_shapes=[pltpu.VMEM((T, HGRN_COLS), jnp.float32),
                        pltpu.VMEM((HGRN_HPS, HEAD_DIM, HEAD_DIM), jnp.float32),
                        pltpu.VMEM((tile_rows, HGRN_COLS), jnp.float32),
                        pltpu.VMEM((tile_rows, HGRN_COLS), jnp.bfloat16),
                        pltpu.VMEM((tile_rows, HGRN_COLS), jnp.bfloat16),
                        pltpu.VMEM((tile_rows // HGRN_CHUNK * 8, HGRN_COLS), jnp.float32)],
        compiler_params=pltpu.CompilerParams(
            dimension_semantics=("arbitrary", "arbitrary", "arbitrary"),
            vmem_limit_bytes=VMEM_LIMIT_BYTES,
        ),
        name="hgrn2",
    )(z, z, z, z, z, lb.astype(jnp.float32), norm_g.reshape(1, HEAD_DIM).astype(jnp.float32),
      jnp.asarray(dmat, jnp.bfloat16), jnp.asarray(amask, jnp.float32))


NA_KEYS = NA_WIN_R * GRID_W


def _na_bias_table(rpb):
    qc = np.arange(GRID_W)[:, None]
    kc = np.arange(GRID_W)[None, :]
    ws = np.clip(qc - NA_WIN_C // 2, 0, GRID_W - NA_WIN_C)
    col_ok = (kc >= ws) & (kc < ws + NA_WIN_C)
    dc = np.clip(kc - qc + NA_WIN_C - 1, 0, 2 * NA_WIN_C - 2)
    toe = jnp.where(jnp.asarray(col_ok)[None, None], rpb.astype(jnp.float32)[:, :, dc], NEG_INF)
    cases = []
    for c in range(NA_WIN_R):
        cases.append(jnp.concatenate([toe[:, c + j] for j in range(NA_WIN_R)], axis=-1))
    return jnp.stack(cases, axis=1)


def _na_kernel(q_ref, k_ref, v_ref, bias_ref, qg_ref, kg_ref, o_ref, kn_ref, *, rows, tile_grid_rows):
    qt = pl.program_id(2)

    @pl.when(qt == 0)
    def _():
        def kbody(i, carry):
            blk = pl.ds(pl.multiple_of(i * NA_KEYS, NA_KEYS), NA_KEYS)
            kn_ref[blk, :] = _bf16(_rms(k_ref[blk, :], kg_ref[...]))
            return carry

        lax.fori_loop(0, rows * GRID_W // NA_KEYS, kbody, 0)

    def body(rl, carry):
        r = qt * tile_grid_rows + rl
        rs = jnp.clip(r - NA_WIN_R // 2, 0, rows - NA_WIN_R)
        q0 = pl.multiple_of(rl * GRID_W, GRID_W)
        k0 = pl.multiple_of(rs * GRID_W, GRID_W)
        q = _rms(q_ref[pl.ds(q0, GRID_W), :], qg_ref[...]) * (1.0 / np.sqrt(HEAD_DIM))
        s = _dot_nt(q, kn_ref[pl.ds(k0, NA_KEYS), :]) + bias_ref[rs - r + NA_WIN_R - 1]
        m = jnp.max(s, axis=-1, keepdims=True)
        p = jnp.exp(s - m)
        den = jnp.sum(p, axis=-1, keepdims=True)
        o_ref[pl.ds(q0, GRID_W), :] = _dot(p, v_ref[pl.ds(k0, NA_KEYS), :]) / den
        return carry

    lax.fori_loop(0, tile_grid_rows, body, 0, unroll=4)


def _neighborhood_mixer(z, rpb, qnorm_g, knorm_g):
    B, T, _ = z.shape
    rows = T // GRID_W
    assert rows >= NA_WIN_R
    tile_grid_rows = min(rows, 32)
    tq = tile_grid_rows * GRID_W
    bias = _na_bias_table(rpb)
    gq = qnorm_g.reshape(1, HEAD_DIM).astype(jnp.float32)
    gk = knorm_g.reshape(1, HEAD_DIM).astype(jnp.float32)
    return pl.pallas_call(
        functools.partial(_na_kernel, rows=rows, tile_grid_rows=tile_grid_rows),
        out_shape=jax.ShapeDtypeStruct((B, T, B_WIDTH), jnp.float32),
        grid=(B, B_HEADS, T // tq),
        in_specs=[
            pl.BlockSpec((None, tq, HEAD_DIM), lambda b, h, t: (b, t, B_COL0 + h)),
            pl.BlockSpec((None, T, HEAD_DIM), lambda b, h, t: (b, 0, B_COL0 + B_HEADS + h)),
            pl.BlockSpec((None, T, HEAD_DIM), lambda b, h, t: (b, 0, B_COL0 + 2 * B_HEADS + h)),
            pl.BlockSpec((None, NA_WIN_R, GRID_W, NA_KEYS), lambda b, h, t: (h, 0, 0, 0)),
            pl.BlockSpec((1, HEAD_DIM), lambda b, h, t: (0, 0)),
            pl.BlockSpec((1, HEAD_DIM), lambda b, h, t: (0, 0)),
        ],
        out_specs=pl.BlockSpec((None, tq, HEAD_DIM), lambda b, h, t: (b, t, h)),
        scratch_shapes=[pltpu.VMEM((T, HEAD_DIM), jnp.bfloat16)],
        compiler_params=pltpu.CompilerParams(
            dimension_semantics=("arbitrary", "arbitrary", "arbitrary"),
            vmem_limit_bytes=VMEM_LIMIT_BYTES,
        ),
        name="neighborhood_attn",
    )(z, z, z, bias, gq, gk)


DIL_HALF = 64
DIL_QB = 128
assert all(w // (2 * d) == DIL_HALF for w, d in DIL_GROUPS)


def _rope_tables(T):
    half = HEAD_DIM // 2
    inv = 1.0 / (ROPE_THETA ** (jnp.arange(half, dtype=jnp.float32) * 2.0 / HEAD_DIM))
    ang = jnp.arange(T, dtype=jnp.float32)[:, None] * inv[None, :]
    cos, sin = jnp.cos(ang), jnp.sin(ang)
    return jnp.concatenate([cos, cos], axis=-1), jnp.concatenate([-sin, sin], axis=-1)


def _dil_kernel(q_ref, k_ref, v_ref, qcos_ref, qsin_ref, kcos_ref, ksin_ref, qg_ref, kg_ref,
                o_ref, lse_ref, kn_ref, *, L, Lq, KW, QB, KB):
    qi = pl.program_id(3)

    def norm_rope(x, g, cos, sin):
        xn = _rms(x, g)
        return xn * cos + pltpu.roll(xn, HEAD_DIM // 2, 1) * sin

    @pl.when(qi == 0)
    def _():
        def kbody(i, carry):
            r = pl.ds(pl.multiple_of(i * KB, KB), KB)
            kn_ref[r, :] = _bf16(norm_rope(k_ref[r, :], kg_ref[...], kcos_ref[r, :], ksin_ref[r, :]))
            return carry

        lax.fori_loop(0, L // KB, kbody, 0)

    def body(i, carry):
        q0l = pl.multiple_of(i * QB, QB)
        q0 = qi * Lq + q0l
        kb = pl.multiple_of(jnp.clip(q0 - DIL_HALF, 0, L - KW), DIL_HALF)
        rows = pl.ds(q0l, QB)
        q = norm_rope(q_ref[rows, :], qg_ref[...], qcos_ref[rows, :], qsin_ref[rows, :])
        s = _dot_nt(q * (1.0 / np.sqrt(HEAD_DIM)), kn_ref[pl.ds(kb, KW), :])
        qpos = q0 + lax.broadcasted_iota(jnp.int32, (QB, KW), 0)
        kpos = kb + lax.broadcasted_iota(jnp.int32, (QB, KW), 1)
        s = jnp.where(jnp.abs(qpos - kpos) <= DIL_HALF, s, NEG_INF)
        m = jnp.max(s, axis=-1, keepdims=True)
        p = jnp.exp(s - m)
        den = jnp.sum(p, axis=-1, keepdims=True)
        o_ref[rows, :] = _dot(p, v_ref[pl.ds(kb, KW), :]) / den
        lse_ref[rows, :] = jnp.broadcast_to(m + jnp.log(den), (QB, HEAD_DIM))
        return carry

    lax.fori_loop(0, Lq // QB, body, 0, unroll=2 if Lq // QB >= 2 else 1)


C_GROUP_COLS = 3 * C_HEADS_PER_GROUP * HEAD_DIM


def _c_proj_kernel(x_ref, g_ref, w_ref, o0_ref, o1_ref, o2_ref, xn_ref, res_ref, *, tm):
    gi = pl.program_id(1)

    @pl.when(gi == 0)
    def _():
        xn_ref[...] = _bf16(_rms(x_ref[...], g_ref[...]))

    res = jnp.dot(xn_ref[...], w_ref[...], preferred_element_type=jnp.float32)
    for g, ((_, d), o_ref) in enumerate(zip(DIL_GROUPS, (o0_ref, o1_ref, o2_ref))):
        @pl.when(gi == g)
        def _(d=d, o_ref=o_ref):
            if d == 1:
                o_ref[...] = res
            else:
                for c in range(C_GROUP_COLS // LANES):
                    res_ref[c] = res[:, c * LANES:(c + 1) * LANES]
                for j in range(d):
                    for c in range(C_GROUP_COLS // LANES):
                        c0 = j * C_GROUP_COLS + c * LANES
                        o_ref[:, c0:c0 + LANES] = res_ref[c, pl.ds(j, tm // d, stride=d), :]


def _c_proj(hf, gain, w_c, *, tm=512):
    n, k = hf.shape
    hp = C_HEADS_PER_GROUP
    cols = []
    for g in range(len(DIL_GROUPS)):
        for part in range(3):
            c0 = part * C_WIDTH + g * hp * HEAD_DIM
            cols.append(w_c[:, c0:c0 + hp * HEAD_DIM])
    w_perm = jnp.concatenate(cols, axis=1).astype(jnp.bfloat16)
    return pl.pallas_call(
        functools.partial(_c_proj_kernel, tm=tm),
        out_shape=tuple(jax.ShapeDtypeStruct((n // d, d * C_GROUP_COLS), jnp.float32) for _, d in DIL_GROUPS),
        grid=(n // tm, len(DIL_GROUPS)),
        in_specs=[pl.BlockSpec((tm, k), lambda i, g: (i, 0)),
                  pl.BlockSpec((1, k), lambda i, g: (0, 0)),
                  pl.BlockSpec((k, C_GROUP_COLS), lambda i, g: (0, g))],
        out_specs=tuple(pl.BlockSpec((tm // d, d * C_GROUP_COLS), lambda i, g: (i, 0)) for _, d in DIL_GROUPS),
        scratch_shapes=[pltpu.VMEM((tm, k), jnp.bfloat16),
                        pltpu.VMEM((C_GROUP_COLS // LANES, tm, LANES), jnp.float32)],
        compiler_params=pltpu.CompilerParams(dimension_semantics=("arbitrary", "arbitrary"),
                                             vmem_limit_bytes=VMEM_LIMIT_BYTES),
        name="c_proj",
    )(hf, gain.reshape(1, k).astype(jnp.float32), w_perm)


def _dilated_group(zg, B, T, d, qnorm_g, knorm_g, cos2, sin2):
    hp = C_HEADS_PER_GROUP
    L = T // d
    Lq = min(L, 1024)
    QB = min(DIL_QB, Lq)
    KW = min(2 * DIL_QB, L)
    KB = min(L, 512)
    zv = zg.reshape(B, L, d * C_GROUP_COLS)
    cosv = cos2.reshape(L, d * HEAD_DIM)
    sinv = sin2.reshape(L, d * HEAD_DIM)
    gq = qnorm_g.reshape(1, HEAD_DIM).astype(jnp.float32)
    gk = knorm_g.reshape(1, HEAD_DIM).astype(jnp.float32)

    def zcol(part):
        return lambda b, j, h, qi: (j * 3 + part) * hp + h

    out_sds = jax.ShapeDtypeStruct((B, L, d * hp * HEAD_DIM), jnp.float32)
    out_spec = pl.BlockSpec((None, Lq, HEAD_DIM), lambda b, j, h, qi: (b, qi, j * hp + h))
    o, lse = pl.pallas_call(
        functools.partial(_dil_kernel, L=L, Lq=Lq, KW=KW, QB=QB, KB=KB),
        out_shape=(out_sds, out_sds),
        grid=(B, d, hp, L // Lq),
        in_specs=[
            pl.BlockSpec((None, Lq, HEAD_DIM), lambda b, j, h, qi: (b, qi, zcol(0)(b, j, h, qi))),
            pl.BlockSpec((None, L, HEAD_DIM), lambda b, j, h, qi: (b, 0, zcol(1)(b, j, h, qi))),
            pl.BlockSpec((None, L, HEAD_DIM), lambda b, j, h, qi: (b, 0, zcol(2)(b, j, h, qi))),
            pl.BlockSpec((Lq, HEAD_DIM), lambda b, j, h, qi: (qi, j)),
            pl.BlockSpec((Lq, HEAD_DIM), lambda b, j, h, qi: (qi, j)),
            pl.BlockSpec((L, HEAD_DIM), lambda b, j, h, qi: (0, j)),
            pl.BlockSpec((L, HEAD_DIM), lambda b, j, h, qi: (0, j)),
            pl.BlockSpec((1, HEAD_DIM), lambda b, j, h, qi: (0, 0)),
            pl.BlockSpec((1, HEAD_DIM), lambda b, j, h, qi: (0, 0)),
        ],
        out_specs=(out_spec, out_spec),
        scratch_shapes=[pltpu.VMEM((L, HEAD_DIM), jnp.bfloat16)],
        compiler_params=pltpu.CompilerParams(
            dimension_semantics=("arbitrary", "arbitrary", "arbitrary", "arbitrary"),
            vmem_limit_bytes=VMEM_LIMIT_BYTES,
        ),
        name=f"dilated_attn_d{d}",
    )(zv, zv, zv, cosv, sinv, cosv, sinv, gq, gk)
    return o.reshape(B * L, d * hp * HEAD_DIM), lse.reshape(B * L, d * hp * HEAD_DIM)


def _dilated_mixer(zgs, B, T, qnorm_g, knorm_g):
    cos2, sin2 = _rope_tables(T)
    outs, lses = [], []
    for zg, (_, d) in zip(zgs, DIL_GROUPS):
        o, lse = _dilated_group(zg, B, T, d, qnorm_g, knorm_g, cos2, sin2)
        outs.append(o)
        lses.append(lse)
    return outs, lses


def _out_proj_kernel(h_ref, a_ref, b_ref, o0_ref, o1_ref, o2_ref, l0_ref, l1_ref, l2_ref, w_ref,
                     g2_ref, wr_ref, out_ref, xn_ref, aff_ref, mix_ref, stage_ref, hrow_ref, *, tm, tn):
    gw = C_HEADS_PER_GROUP * HEAD_DIM
    n_col = D_MODEL // tn

    def token_major(src_ref, d, slot):
        if d == 1:
            return src_ref[...]
        nc = gw // LANES
        for j in range(d):
            for c in range(nc):
                c0 = j * gw + c * LANES
                stage_ref[slot * nc + c, pl.ds(j, tm // d, stride=d), :] = src_ref[:, c0:c0 + LANES]
        return jnp.concatenate([stage_ref[slot * nc + c] for c in range(nc)], axis=1)

    @pl.when(pl.program_id(1) == 0)
    def _():
        mix_ref[:, :A_WIDTH] = _bf16(a_ref[...])
        mix_ref[:, A_WIDTH:A_WIDTH + B_WIDTH] = _bf16(b_ref[...])
        dils = [d for _, d in DIL_GROUPS]
        lses = [token_major(l_ref, d, g) for g, (l_ref, d) in enumerate(zip((l0_ref, l1_ref, l2_ref), dils))]
        m = jnp.maximum(jnp.maximum(lses[0], lses[1]), lses[2])
        es = [jnp.exp(l - m) for l in lses]
        inv = 1.0 / (es[0] + es[1] + es[2])
        for g, (o_ref, d) in enumerate(zip((o0_ref, o1_ref, o2_ref), dils)):
            c0 = A_WIDTH + B_WIDTH + g * gw
            mix_ref[:, c0:c0 + gw] = _bf16(token_major(o_ref, d, len(dils) + g) * (es[g] * inv))

    j = pl.program_id(1)
    h_new = h_ref[...] + jnp.dot(mix_ref[...], w_ref[...], preferred_element_type=jnp.float32)
    out_ref[...] = h_new
    for jj in range(n_col):
        @pl.when(j == jj)
        def _(jj=jj):
            hrow_ref[:, jj * tn:(jj + 1) * tn] = h_new

    @pl.when(j == n_col - 1)
    def _():
        xn = _rms(hrow_ref[...], g2_ref[...])
        xn_ref[...] = _bf16(xn)
        logits = jnp.dot(xn, wr_ref[...], preferred_element_type=jnp.float32, precision=lax.Precision.HIGHEST)
        lane = lax.broadcasted_iota(jnp.int32, logits.shape, 1)
        logits = jnp.where(lane < N_EXPERTS, logits, NEG_INF)
        e = jnp.exp(logits - jnp.max(logits, axis=-1, keepdims=True))
        aff_ref[...] = e / jnp.sum(e, axis=-1, keepdims=True)


def _out_proj_router(hf, a_out, b_out, c_outs, c_lses, w_out, norm2_g, w_router, *, tm=512, tn=512):
    n = hf.shape[0]
    gw = C_HEADS_PER_GROUP * HEAD_DIM
    row = lambda width: pl.BlockSpec((tm, width), lambda i, j: (i, 0))
    dil = [pl.BlockSpec((tm // d, d * gw), lambda i, j: (i, 0)) for _, d in DIL_GROUPS]
    wr = jnp.zeros((D_MODEL, LANES), jnp.float32).at[:, :N_EXPERTS].set(w_router.astype(jnp.float32))
    return pl.pallas_call(
        functools.partial(_out_proj_kernel, tm=tm, tn=tn),
        out_shape=(jax.ShapeDtypeStruct((n, D_MODEL), jnp.float32),
                   jax.ShapeDtypeStruct((n, D_MODEL), jnp.bfloat16),
                   jax.ShapeDtypeStruct((n, LANES), jnp.float32)),
        grid=(n // tm, D_MODEL // tn),
        in_specs=[pl.BlockSpec((tm, tn), lambda i, j: (i, j)), row(A_WIDTH), row(B_WIDTH)]
        + dil + dil + [pl.BlockSpec((D_MODEL, tn), lambda i, j: (0, j)),
                       pl.BlockSpec((1, D_MODEL), lambda i, j: (0, 0)),
                       pl.BlockSpec((D_MODEL, LANES), lambda i, j: (0, 0))],
        out_specs=(pl.BlockSpec((tm, tn), lambda i, j: (i, j)), row(D_MODEL), row(LANES)),
        scratch_shapes=[pltpu.VMEM((tm, D_MODEL), jnp.bfloat16),
                        pltpu.VMEM((2 * len(DIL_GROUPS) * gw // LANES, tm, LANES), jnp.float32),
                        pltpu.VMEM((tm, D_MODEL), jnp.float32)],
        compiler_params=pltpu.CompilerParams(
            dimension_semantics=("arbitrary", "arbitrary"),
            vmem_limit_bytes=VMEM_LIMIT_BYTES,
        ),
        name="out_proj_router",
    )(hf, a_out.reshape(n, A_WIDTH), b_out.reshape(n, B_WIDTH), *c_outs, *c_lses,
      w_out.astype(jnp.bfloat16), norm2_g.reshape(1, D_MODEL).astype(jnp.float32), wr)


LANES = 128
FFN_ROWS = 1024
HALF_D = D_MODEL // 2
SLAB_SUB = HALF_D // LANES
COMBINE_ROWS = 512


def _pack_bf16_pairs(lo, hi):
    bits = lambda v: lax.bitcast_convert_type(_bf16(v).astype(jnp.float32), jnp.uint32)
    return (bits(lo) >> 16) | (bits(hi) & jnp.uint32(0xFFFF0000))


def _unpack_bf16_pairs(words):
    return (lax.bitcast_convert_type(words << 16, jnp.float32),
            lax.bitcast_convert_type(words & jnp.uint32(0xFFFF0000), jnp.float32))


def _expert_kernel(dest_hbm, x_ref, g_ref, wg_ref, wu_ref, wd_ref, y_hbm, ybuf, dbuf, ysem, dsem,
                   *, n_steps, steps_per_expert):
    step = pl.program_id(0) * steps_per_expert + pl.program_id(1)
    slot = step % 2

    def dest_copy(s, sl):
        return pltpu.make_async_copy(dest_hbm.at[s], dbuf.at[sl], dsem.at[sl])

    def scatter_wait(sl):
        pltpu.make_async_copy(ybuf.at[sl], ybuf.at[sl], ysem.at[sl]).wait()

    @pl.when(step == 0)
    def _():
        dest_copy(0, 0).start()

    @pl.when(step + 1 < n_steps)
    def _():
        dest_copy(step + 1, 1 - slot).start()

    @pl.when(step >= 2)
    def _():
        scatter_wait(slot)

    x = x_ref[...]
    gate = jnp.dot(x, wg_ref[...], preferred_element_type=jnp.float32)
    hdn = gate * _sigmoid(gate) * jnp.dot(x, wu_ref[...], preferred_element_type=jnp.float32)
    y = jnp.dot(_bf16(hdn), wd_ref[...], preferred_element_type=jnp.float32)
    y = y * jnp.concatenate([g_ref[...]] * (D_MODEL // LANES), axis=1)
    words = _pack_bf16_pairs(y[:, :HALF_D], y[:, HALF_D:])
    for sub in range(SLAB_SUB):
        ybuf[slot, pl.ds(sub, FFN_ROWS, stride=SLAB_SUB), :] = words[:, sub * LANES:(sub + 1) * LANES]

    dest_copy(step, slot).wait()
    for r in range(FFN_ROWS):
        pltpu.make_async_copy(ybuf.at[slot, pl.ds(r * SLAB_SUB, SLAB_SUB), :],
                              y_hbm.at[pl.ds(pl.multiple_of(dbuf[slot, r], SLAB_SUB), SLAB_SUB), :],
                              ysem.at[slot]).start()

    @pl.when(step == n_steps - 1)
    def _():
        scatter_wait(slot)
        if n_steps >= 2:
            scatter_wait(1 - slot)


def _expert_ffn(xe, gates_rep, dest, w_gate, w_up, w_down, n_tokens):
    rows, d = xe.shape
    cap = rows // N_EXPERTS
    f = w_gate.shape[-1]
    spe = cap // FFN_ROWS
    n_steps = N_EXPERTS * spe
    return pl.pallas_call(
        functools.partial(_expert_kernel, n_steps=n_steps, steps_per_expert=spe),
        out_shape=jax.ShapeDtypeStruct((N_EXPERTS * n_tokens * SLAB_SUB, LANES), jnp.uint32),
        grid=(N_EXPERTS, spe),
        in_specs=[
            pl.BlockSpec(memory_space=pl.ANY),
            pl.BlockSpec((FFN_ROWS, d), lambda e, t: (e * spe + t, 0)),
            pl.BlockSpec((FFN_ROWS, LANES), lambda e, t: (e * spe + t, 0)),
            pl.BlockSpec((None, d, f), lambda e, t: (e, 0, 0), pipeline_mode=pl.Buffered(1)),
            pl.BlockSpec((None, d, f), lambda e, t: (e, 0, 0), pipeline_mode=pl.Buffered(1)),
            pl.BlockSpec((None, f, d), lambda e, t: (e, 0, 0), pipeline_mode=pl.Buffered(1)),
        ],
        out_specs=pl.BlockSpec(memory_space=pl.ANY),
        scratch_shapes=[pltpu.VMEM((2, FFN_ROWS * SLAB_SUB, LANES), jnp.uint32),
                        pltpu.SMEM((2, FFN_ROWS), jnp.int32),
                        pltpu.SemaphoreType.DMA((2,)),
                        pltpu.SemaphoreType.DMA((2,))],
        compiler_params=pltpu.CompilerParams(dimension_semantics=("arbitrary", "arbitrary"),
                                             vmem_limit_bytes=VMEM_LIMIT_BYTES),
        name="expert_ffn",
    )((dest * SLAB_SUB).reshape(n_steps, FFN_ROWS), xe, gates_rep, w_gate.astype(jnp.bfloat16),
      w_up.astype(jnp.bfloat16), w_down.astype(jnp.bfloat16))


COMBINE_RANKS = 4
COMBINE_STEPS = N_EXPERTS // COMBINE_RANKS
PLE_TN = 512
PLE_STEPS = D_MODEL // PLE_TN


def _combine_ple_kernel(tmax_ref, h_ref, cnt_ref, *refs):
    y_refs = refs[:COMBINE_RANKS]
    (p_ref, gg_ref, pg_ref, wp_ref, wg_ref, out_ref, h2_ref, xn_ref, e_ref) = refs[COMBINE_RANKS:]
    i, step = pl.program_id(0), pl.program_id(1)
    subs_per_tile = PLE_TN // LANES

    @pl.when(step == 0)
    def _():
        for jj in range(PLE_STEPS):
            h2_ref[jj] = h_ref[:, jj * PLE_TN:(jj + 1) * PLE_TN]

    for q, y_ref in enumerate(y_refs):
        @pl.when((step < COMBINE_STEPS) & (step * COMBINE_RANKS + q < tmax_ref[i]))
        def _(q=q, y_ref=y_ref):
            chosen = cnt_ref[...] > step * COMBINE_RANKS + q
            for sub in range(SLAB_SUB):
                halves = _unpack_bf16_pairs(y_ref[pl.ds(sub, COMBINE_ROWS, stride=SLAB_SUB), :])
                cols = slice((sub % subs_per_tile) * LANES, (sub % subs_per_tile + 1) * LANES)
                for hh, half in enumerate(halves):
                    tile = hh * (HALF_D // PLE_TN) + sub // subs_per_tile
                    h2_ref[tile, :, cols] += jnp.where(chosen, half, 0.0)

    @pl.when(step == COMBINE_STEPS)
    def _():
        h2 = jnp.concatenate([h2_ref[jj] for jj in range(PLE_STEPS)], axis=1)
        xn_ref[...] = _bf16(_rms(h2, gg_ref[...]))
        e = _rms(jnp.dot(_bf16(p_ref[...]), wp_ref[...], preferred_element_type=jnp.float32), pg_ref[...])
        for jj in range(PLE_STEPS):
            e_ref[jj] = e[:, jj * PLE_TN:(jj + 1) * PLE_TN]

    @pl.when(step >= COMBINE_STEPS)
    def _():
        j = step - COMBINE_STEPS
        gate = _sigmoid(jnp.dot(xn_ref[...], wg_ref[...], preferred_element_type=jnp.float32))
        out_ref[...] = h2_ref[j] + gate * e_ref[j]


def _combine_ple(hf, slabs, cnt_rep, tile_max, p, w_ple, ple_norm_g, gate_norm_g, w_ple_gate):
    n, d = hf.shape
    tm = COMBINE_ROWS
    pd = p.shape[1]
    slabs = slabs.reshape(N_EXPERTS, n * SLAB_SUB, LANES)
    vec = lambda g: g.reshape(1, D_MODEL).astype(jnp.float32)

    def slab_spec(q):
        return pl.BlockSpec((None, tm * SLAB_SUB, LANES), lambda i, s, tmax: (
            jnp.minimum(jnp.minimum(s, COMBINE_STEPS - 1) * COMBINE_RANKS + q, jnp.maximum(tmax[i], 1) - 1), i, 0))

    col = lambda i, s, tmax: jnp.maximum(s - COMBINE_STEPS, 0)
    return pl.pallas_call(
        _combine_ple_kernel,
        out_shape=jax.ShapeDtypeStruct((n, d), jnp.float32),
        grid_spec=pltpu.PrefetchScalarGridSpec(
            num_scalar_prefetch=1,
            grid=(n // tm, COMBINE_STEPS + PLE_STEPS),
            in_specs=[pl.BlockSpec((tm, d), lambda i, s, tmax: (i, 0)),
                      pl.BlockSpec((tm, LANES), lambda i, s, tmax: (i, 0))]
            + [slab_spec(q) for q in range(COMBINE_RANKS)]
            + [pl.BlockSpec((tm, pd), lambda i, s, tmax: (i, 0)),
               pl.BlockSpec((1, d), lambda i, s, tmax: (0, 0)),
               pl.BlockSpec((1, d), lambda i, s, tmax: (0, 0)),
               pl.BlockSpec((pd, d), lambda i, s, tmax: (0, 0)),
               pl.BlockSpec((d, PLE_TN), lambda i, s, tmax: (0, col(i, s, tmax)))],
            out_specs=pl.BlockSpec((tm, PLE_TN), lambda i, s, tmax: (i, col(i, s, tmax))),
            scratch_shapes=[pltpu.VMEM((PLE_STEPS, tm, PLE_TN), jnp.float32),
                            pltpu.VMEM((tm, d), jnp.bfloat16),
                            pltpu.VMEM((PLE_STEPS, tm, PLE_TN), jnp.float32)],
        ),
        compiler_params=pltpu.CompilerParams(dimension_semantics=("arbitrary", "arbitrary"),
                                             vmem_limit_bytes=VMEM_LIMIT_BYTES),
        name="ffn_combine_ple",
    )(tile_max, hf, cnt_rep, *([slabs] * COMBINE_RANKS), p, vec(gate_norm_g), vec(ple_norm_g),
      w_ple.astype(jnp.bfloat16), w_ple_gate.astype(jnp.bfloat16))


def _expert_choice_slabs(xn, aff, w_gate, w_up, w_down):
    n = xn.shape[0]
    cap = (EC_CAPACITY_FACTOR * n) // N_EXPERTS
    gates, idx = lax.top_k(aff[:, :N_EXPERTS].T, cap)
    xe = xn[idx.reshape(-1)]
    chosen = jnp.zeros((N_EXPERTS, n), jnp.int32).at[jnp.arange(N_EXPERTS)[:, None], idx].set(1)
    rank = jnp.cumsum(chosen, axis=0) - chosen
    dest = jnp.take_along_axis(rank, idx, axis=1) * n + idx
    cnt = jnp.sum(chosen, axis=0)
    slabs = _expert_ffn(xe, jnp.broadcast_to(gates.reshape(-1, 1), (N_EXPERTS * cap, LANES)),
                        dest.reshape(-1).astype(jnp.int32), w_gate, w_up, w_down, n)
    cnt_rep = jnp.broadcast_to(cnt[:, None], (n, LANES)).astype(jnp.int32)
    tile_max = jnp.max(cnt.reshape(n // COMBINE_ROWS, COMBINE_ROWS), axis=1).astype(jnp.int32)
    return slabs, cnt_rep, tile_max


def _layer(h, p_i, lb_f, lb_b, norm1_g, w_in, a_norm_g, b_qnorm_g, b_knorm_g, b_rpb,
           c_qnorm_g, c_knorm_g, w_out, norm2_g, w_router, w_gate, w_up, w_down,
           w_ple, ple_norm_g, gate_norm_g, w_ple_gate):
    B, T, _ = h.shape
    N = B * T
    hf = h.reshape(N, D_MODEL)
    ab_width = IN_WIDTH - 3 * C_WIDTH
    z_ab = _matmul(hf, w_in[:, :ab_width], norm1_g, tn=768).reshape(B, T, ab_width)
    z_c = _c_proj(hf, norm1_g, w_in[:, ab_width:])
    a_out = _hgrn2_mixer(z_ab, lb_f, lb_b, a_norm_g)
    b_out = _neighborhood_mixer(z_ab, b_rpb, b_qnorm_g, b_knorm_g)
    c_outs, c_lses = _dilated_mixer(z_c, B, T, c_qnorm_g, c_knorm_g)
    hf, xn, aff = _out_proj_router(hf, a_out, b_out, c_outs, c_lses, w_out, norm2_g, w_router)
    slabs, cnt_rep, tile_max = _expert_choice_slabs(xn, aff, w_gate, w_up, w_down)
    out = _combine_ple(hf, slabs, cnt_rep, tile_max, p_i.reshape(N, -1), w_ple, ple_norm_g, gate_norm_g,
                       w_ple_gate)
    return out.reshape(B, T, D_MODEL)


def _trunk(h, p, lb, weights):
    (norm1_g, w_in, a_norm_g, b_qnorm_g, b_knorm_g, b_rpb, c_qnorm_g, c_knorm_g, w_out,
     norm2_g, w_router, w_gate, w_up, w_down, w_ple, ple_norm_g, gate_norm_g, w_ple_gate) = weights
    for i in range(DEPTH):
        h = _layer(h, p[i], lb[0, i], lb[1, i], norm1_g[i], w_in[i], a_norm_g[i], b_qnorm_g[i],
                   b_knorm_g[i], b_rpb[i], c_qnorm_g[i], c_knorm_g[i], w_out[i], norm2_g[i],
                   w_router[i], w_gate[i], w_up[i], w_down[i], w_ple[i], ple_norm_g[i],
                   gate_norm_g[i], w_ple_gate[i])
    return h


def kernel(x_prompt, x_sample, p_prompt, p_sample, norm1_g, w_in, lb_logits, a_norm_g,
           b_qnorm_g, b_knorm_g, b_rpb, c_qnorm_g, c_knorm_g, w_out, norm2_g, w_router,
           w_gate, w_up, w_down, w_ple, ple_norm_g, gate_norm_g, w_ple_gate):
    pr = jax.nn.softmax(lb_logits.astype(jnp.float32), axis=1)
    lb = jnp.cumsum(pr, axis=1) - pr[:, :1]
    weights = (norm1_g, w_in, a_norm_g, b_qnorm_g, b_knorm_g, b_rpb, c_qnorm_g, c_knorm_g, w_out,
               norm2_g, w_router, w_gate, w_up, w_down, w_ple, ple_norm_g, gate_norm_g, w_ple_gate)
    y_prompt = _trunk(x_prompt, p_prompt, lb, weights)
    y_sample = _trunk(x_sample, p_sample, lb, weights)
    return (y_prompt, y_sample)
```

```python
import functools

import jax
import jax.numpy as jnp
import numpy as np
from jax import lax
from jax.experimental import pallas as pl
from jax.experimental.pallas import tpu as pltpu

D_MODEL = 2048
DEPTH = 4
HEAD_DIM = 128
A_HEADS = 6
B_HEADS = 4
C_HEADS = 6
A_WIDTH = A_HEADS * HEAD_DIM
B_WIDTH = B_HEADS * HEAD_DIM
C_WIDTH = C_HEADS * HEAD_DIM
IN_WIDTH = 5 * A_WIDTH + 3 * B_WIDTH + 3 * C_WIDTH
HGRN_CHUNK = 128
GRID_W = 64
NA_WIN_R = 8
NA_WIN_C = 16
DIL_GROUPS = ((128, 1), (512, 4), (2048, 16))
C_HEADS_PER_GROUP = C_HEADS // len(DIL_GROUPS)
ROPE_THETA = 10000.0
N_EXPERTS = 16
EC_CAPACITY_FACTOR = 2
EPS = 1e-6

VMEM_LIMIT_BYTES = 56 * 1024 * 1024
NEG_INF = -1e30
Z_COLS = IN_WIDTH // HEAD_DIM
B_COL0 = 5 * A_HEADS
C_COL0 = B_COL0 + 3 * B_HEADS


def _sigmoid(x):
    return 1.0 / (1.0 + jnp.exp(-x))


def _bf16(x):
    return x.astype(jnp.bfloat16)


def _dot(a, b):
    return jnp.dot(_bf16(a), _bf16(b), preferred_element_type=jnp.float32)


def _dot_nt(a, b):
    return lax.dot_general(_bf16(a), _bf16(b), (((1,), (1,)), ((), ())),
                           preferred_element_type=jnp.float32)


def _dot_tn(a, b):
    return lax.dot_general(_bf16(a), _bf16(b), (((0,), (0,)), ((), ())),
                           preferred_element_type=jnp.float32)


def _rms(x, g):
    return x * lax.rsqrt(jnp.mean(x * x, axis=-1, keepdims=True) + EPS) * g


def _norm_matmul_kernel(x_ref, g_ref, w_ref, o_ref, xn_ref, *, normalize):
    @pl.when(pl.program_id(1) == 0)
    def _():
        x = x_ref[...]
        if normalize:
            x = _rms(x, g_ref[...])
        xn_ref[...] = x.astype(jnp.bfloat16)

    o_ref[...] = jnp.dot(xn_ref[...], w_ref[...], preferred_element_type=jnp.float32)


def _matmul(x, w, gain=None, *, tm=1024, tn=512):
    n, k = x.shape
    m = w.shape[1]
    tm = min(tm, n)
    tn = min(tn, m)
    assert n % tm == 0 and m % tn == 0
    normalize = gain is not None
    g = (gain if normalize else jnp.ones((k,), jnp.float32)).reshape(1, k).astype(jnp.float32)
    return pl.pallas_call(
        functools.partial(_norm_matmul_kernel, normalize=normalize),
        out_shape=jax.ShapeDtypeStruct((n, m), jnp.float32),
        grid=(n // tm, m // tn),
        in_specs=[
            pl.BlockSpec((tm, k), lambda i, j: (i, 0)),
            pl.BlockSpec((1, k), lambda i, j: (0, 0)),
            pl.BlockSpec((k, tn), lambda i, j: (0, j)),
        ],
        out_specs=pl.BlockSpec((tm, tn), lambda i, j: (i, j)),
        scratch_shapes=[pltpu.VMEM((tm, k), jnp.bfloat16)],
        compiler_params=pltpu.CompilerParams(
            dimension_semantics=("arbitrary", "arbitrary"),
            vmem_limit_bytes=VMEM_LIMIT_BYTES,
        ),
        name="norm_matmul" if normalize else "matmul",
    )(x, g, w.astype(jnp.bfloat16))


HGRN_LEVELS = tuple(HGRN_CHUNK >> (i + 1) for i in range(HGRN_CHUNK.bit_length() - 1))
HGRN_NLEV = len(HGRN_LEVELS)
HGRN_HPS = 2
HGRN_COLS = HGRN_HPS * HEAD_DIM


def _hgrn_tables():
    C = HGRN_CHUNK
    t = np.arange(C)[:, None]
    u = np.arange(C)[None, :]
    dmats, amasks = [], []
    for reverse in (False, True):
        cum = ((u >= t) if reverse else (u <= t)).astype(np.float32)
        blocks, masks = [], []
        for m in HGRN_LEVELS:
            pos = np.arange(C) % (2 * m)
            ref = np.arange(C) - pos + (m if reverse else m - 1)
            blocks.append(cum - cum[ref])
            later = (pos < m) if reverse else (pos >= m)
            same = (t // (2 * m)) == (u // (2 * m))
            masks.append((same & later[:, None] & ~later[None, :]).astype(np.float32))
        last = 0 if reverse else C - 1
        blocks += [cum, cum[last:last + 1] - cum]
        masks.append(np.eye(C, dtype=np.float32))
        assert np.array_equal(sum(masks), cum)
        dmats.append(np.tile(np.concatenate(blocks, axis=0), (1, 2)))
        amasks.append(np.stack(masks, axis=0))
    return np.stack(dmats, axis=0), np.stack(amasks, axis=0)


def _hgrn_intra(q, x, v, lb, dmat, amask, reverse):
    C = HGRN_CHUNK
    f = lb + (1.0 - lb) * _sigmoid(x)
    lf = jnp.log(f)
    k = 1.0 - f
    hi = _bf16(lf)
    mid = _bf16(lf - hi.astype(jnp.float32))
    dall = jnp.dot(dmat[...], jnp.concatenate([hi, mid], axis=0),
                   preferred_element_type=jnp.float32)

    def stack(a):
        return jnp.concatenate([a[:, j * HEAD_DIM:(j + 1) * HEAD_DIM] for j in range(HGRN_HPS)], axis=0)

    def head_blocks(p):
        return [p[j * C:(j + 1) * C, j * C:(j + 1) * C] for j in range(HGRN_HPS)]

    qs, ks = stack(q), stack(k)
    row = lax.broadcasted_iota(jnp.int32, (HGRN_HPS * C, HEAD_DIM), 0)
    a = [amask[HGRN_NLEV] * p for p in head_blocks(_dot_nt(qs, ks))]
    for l, m in enumerate(HGRN_LEVELS):
        later = ((row & m) == 0) if reverse else ((row & m) != 0)
        xh = _bf16(jnp.where(later, qs, ks) * jnp.exp(-jnp.abs(stack(dall[l * C:(l + 1) * C]))))
        pairs = lax.dot_general(xh, xh, (((1,), (1,)), ((), ())), preferred_element_type=jnp.float32)
        a = [a_j + amask[l] * p for a_j, p in zip(a, head_blocks(pairs))]
    oi = jnp.concatenate([_dot(a[j], v[:, j * HEAD_DIM:(j + 1) * HEAD_DIM]) for j in range(HGRN_HPS)],
                         axis=1)
    b = dall[HGRN_NLEV * C:(HGRN_NLEV + 1) * C]
    b_rest = dall[(HGRN_NLEV + 1) * C:(HGRN_NLEV + 2) * C]
    b_last = b[0:1, :] if reverse else b[C - 1:C, :]
    return oi, q * jnp.exp(b), k * jnp.exp(b_rest), jnp.exp(b_last)


def _hgrn_kernel(q_ref, ff_ref, fb_ref, v_ref, g_ref, lb_ref, ng_ref, dmat_ref, amask_ref, o_ref,
                 ofwd_ref, st_ref, oi_ref, qt_ref, kt_ref, eb_ref, *, n_tiles, tile_rows):
    s = pl.program_id(2)
    n_chunks = tile_rows // HGRN_CHUNK
    head_cols = [slice(j * HEAD_DIM, (j + 1) * HEAD_DIM) for j in range(HGRN_HPS)]

    @pl.when((s == 0) | (s == n_tiles))
    def _():
        st_ref[...] = jnp.zeros_like(st_ref)

    def intra_pass(f_ref, direction, reverse):
        def body(c, carry):
            rows = pl.ds(pl.multiple_of(c * HGRN_CHUNK, HGRN_CHUNK), HGRN_CHUNK)
            qp = q_ref[rows, :]
            oi, qt, kt, eb = _hgrn_intra(qp * _sigmoid(qp), f_ref[rows, :], v_ref[rows, :],
                                         lb_ref[direction:direction + 1, :], dmat_ref, amask_ref, reverse)
            oi_ref[rows, :] = oi
            qt_ref[rows, :] = _bf16(qt)
            kt_ref[rows, :] = _bf16(kt)
            eb_ref[pl.ds(pl.multiple_of(c * 8, 8), 8), :] = jnp.broadcast_to(eb, (8, HGRN_COLS))
            return carry

        lax.fori_loop(0, n_chunks, body, 0, unroll=2)

    def state_step(c):
        rows = pl.ds(pl.multiple_of(c * HGRN_CHUNK, HGRN_CHUNK), HGRN_CHUNK)
        outs = []
        for j, cols in enumerate(head_cols):
            st = st_ref[j]
            outs.append(oi_ref[rows, cols] + _dot_nt(qt_ref[rows, cols], st))
            st_ref[j] = st * eb_ref[pl.ds(pl.multiple_of(c * 8, 8), 1), cols] + _dot_tn(v_ref[rows, cols],
                                                                                       kt_ref[rows, cols])
        return rows, outs

    @pl.when(s < n_tiles)
    def _():
        intra_pass(ff_ref, 0, False)
        base = s * tile_rows

        def body(c, carry):
            rows, outs = state_step(c)
            dst = pl.ds(pl.multiple_of(base + c * HGRN_CHUNK, HGRN_CHUNK), HGRN_CHUNK)
            for j, cols in enumerate(head_cols):
                ofwd_ref[dst, cols] = outs[j]
            return carry

        lax.fori_loop(0, n_chunks, body, 0, unroll=2)

    @pl.when(s >= n_tiles)
    def _():
        intra_pass(fb_ref, 1, True)
        base = (2 * n_tiles - 1 - s) * tile_rows

        def body(ci, carry):
            c = n_chunks - 1 - ci
            rows, outs = state_step(c)
            src = pl.ds(pl.multiple_of(base + c * HGRN_CHUNK, HGRN_CHUNK), HGRN_CHUNK)
            for j, cols in enumerate(head_cols):
                y = _rms(outs[j] + ofwd_ref[src, cols], ng_ref[...])
                g = g_ref[rows, cols]
                o_ref[rows, cols] = y * (g * _sigmoid(g))
            return carry

        lax.fori_loop(0, n_chunks, body, 0, unroll=2)


def _hgrn_tile_rows(T):
    return min(T, 2048)


def _hgrn2_mixer(z, lb_f, lb_b, norm_g):
    B, T, _ = z.shape
    tile_rows = _hgrn_tile_rows(T)
    n_tiles = T // tile_rows
    n_groups = A_HEADS // HGRN_HPS
    lb = jnp.stack([lb_f.reshape(n_groups, HGRN_COLS), lb_b.reshape(n_groups, HGRN_COLS)], axis=1)
    last = n_tiles - 1
    dmat, amask = _hgrn_tables()

    def both(s):
        return jnp.where(s < n_tiles, s, 2 * n_tiles - 1 - s)

    def fwd_only(s):
        return jnp.minimum(s, last)

    def bwd_only(s):
        return jnp.where(s < n_tiles, last, 2 * n_tiles - 1 - s)

    def zspec(part, tile_of):
        return pl.BlockSpec((None, tile_rows, HGRN_COLS),
                            lambda b, h, s: (b, tile_of(s), part * n_groups + h))

    return pl.pallas_call(
        functools.partial(_hgrn_kernel, n_tiles=n_tiles, tile_rows=tile_rows),
        out_shape=jax.ShapeDtypeStruct((B, T, A_WIDTH), jnp.float32),
        grid=(B, n_groups, 2 * n_tiles),
        in_specs=[
            zspec(0, both), zspec(1, fwd_only), zspec(2, bwd_only), zspec(3, both), zspec(4, bwd_only),
            pl.BlockSpec((None, 2, HGRN_COLS), lambda b, h, s: (h, 0, 0)),
            pl.BlockSpec((1, HEAD_DIM), lambda b, h, s: (0, 0)),
            pl.BlockSpec((None,) + dmat.shape[1:], lambda b, h, s: (s // n_tiles, 0, 0)),
            pl.BlockSpec((None,) + amask.shape[1:], lambda b, h, s: (s // n_tiles, 0, 0, 0)),
        ],
        out_specs=pl.BlockSpec((None, tile_rows, HGRN_COLS), lambda b, h, s: (b, bwd_only(s), h)),
        scratch_shapes=[pltpu.VMEM((T, HGRN_COLS), jnp.float32),
                        pltpu.VMEM((HGRN_HPS, HEAD_DIM, HEAD_DIM), jnp.float32),
                        pltpu.VMEM((tile_rows, HGRN_COLS), jnp.float32),
                        pltpu.VMEM((tile_rows, HGRN_COLS), jnp.bfloat16),
                        pltpu.VMEM((tile_rows, HGRN_COLS), jnp.bfloat16),
                        pltpu.VMEM((tile_rows // HGRN_CHUNK * 8, HGRN_COLS), jnp.float32)],
        compiler_params=pltpu.CompilerParams(
            dimension_semantics=("arbitrary", "arbitrary", "arbitrary"),
            vmem_limit_bytes=VMEM_LIMIT_BYTES,
        ),
        name="hgrn2",
    )(z, z, z, z, z, lb.astype(jnp.float32), norm_g.reshape(1, HEAD_DIM).astype(jnp.float32),
      jnp.asarray(dmat, jnp.bfloat16), jnp.asarray(amask, jnp.float32))


NA_KEYS = NA_WIN_R * GRID_W


def _na_bias_table(rpb):
    qc = np.arange(GRID_W)[:, None]
    kc = np.arange(GRID_W)[None, :]
    ws = np.clip(qc - NA_WIN_C // 2, 0, GRID_W - NA_WIN_C)
    col_ok = (kc >= ws) & (kc < ws + NA_WIN_C)
    dc = np.clip(kc - qc + NA_WIN_C - 1, 0, 2 * NA_WIN_C - 2)
    toe = jnp.where(jnp.asarray(col_ok)[None, None], rpb.astype(jnp.float32)[:, :, dc], NEG_INF)
    cases = []
    for c in range(NA_WIN_R):
        cases.append(jnp.concatenate([toe[:, c + j] for j in range(NA_WIN_R)], axis=-1))
    return jnp.stack(cases, axis=1)


def _na_kernel(q_ref, k_ref, v_ref, bias_ref, qg_ref, kg_ref, o_ref, kn_ref, *, rows, tile_grid_rows):
    qt = pl.program_id(2)

    @pl.when(qt == 0)
    def _():
        def kbody(i, carry):
            blk = pl.ds(pl.multiple_of(i * NA_KEYS, NA_KEYS), NA_KEYS)
            kn_ref[blk, :] = _bf16(_rms(k_ref[blk, :], kg_ref[...]))
            return carry

        lax.fori_loop(0, rows * GRID_W // NA_KEYS, kbody, 0)

    def body(rl, carry):
        r = qt * tile_grid_rows + rl
        rs = jnp.clip(r - NA_WIN_R // 2, 0, rows - NA_WIN_R)
        q0 = pl.multiple_of(rl * GRID_W, GRID_W)
        k0 = pl.multiple_of(rs * GRID_W, GRID_W)
        q = _rms(q_ref[pl.ds(q0, GRID_W), :], qg_ref[...]) * (1.0 / np.sqrt(HEAD_DIM))
        s = _dot_nt(q, kn_ref[pl.ds(k0, NA_KEYS), :]) + bias_ref[rs - r + NA_WIN_R - 1]
        m = jnp.max(s, axis=-1, keepdims=True)
        p = jnp.exp(s - m)
        den = jnp.sum(p, axis=-1, keepdims=True)
        o_ref[pl.ds(q0, GRID_W), :] = _dot(p, v_ref[pl.ds(k0, NA_KEYS), :]) / den
        return carry

    lax.fori_loop(0, tile_grid_rows, body, 0, unroll=4)


def _neighborhood_mixer(z, rpb, qnorm_g, knorm_g):
    B, T, _ = z.shape
    rows = T // GRID_W
    assert rows >= NA_WIN_R
    tile_grid_rows = min(rows, 32)
    tq = tile_grid_rows * GRID_W
    bias = _na_bias_table(rpb)
    gq = qnorm_g.reshape(1, HEAD_DIM).astype(jnp.float32)
    gk = knorm_g.reshape(1, HEAD_DIM).astype(jnp.float32)
    return pl.pallas_call(
        functools.partial(_na_kernel, rows=rows, tile_grid_rows=tile_grid_rows),
        out_shape=jax.ShapeDtypeStruct((B, T, B_WIDTH), jnp.float32),
        grid=(B, B_HEADS, T // tq),
        in_specs=[
            pl.BlockSpec((None, tq, HEAD_DIM), lambda b, h, t: (b, t, B_COL0 + h)),
            pl.BlockSpec((None, T, HEAD_DIM), lambda b, h, t: (b, 0, B_COL0 + B_HEADS + h)),
            pl.BlockSpec((None, T, HEAD_DIM), lambda b, h, t: (b, 0, B_COL0 + 2 * B_HEADS + h)),
            pl.BlockSpec((None, NA_WIN_R, GRID_W, NA_KEYS), lambda b, h, t: (h, 0, 0, 0)),
            pl.BlockSpec((1, HEAD_DIM), lambda b, h, t: (0, 0)),
            pl.BlockSpec((1, HEAD_DIM), lambda b, h, t: (0, 0)),
        ],
        out_specs=pl.BlockSpec((None, tq, HEAD_DIM), lambda b, h, t: (b, t, h)),
        scratch_shapes=[pltpu.VMEM((T, HEAD_DIM), jnp.bfloat16)],
        compiler_params=pltpu.CompilerParams(
            dimension_semantics=("arbitrary", "arbitrary", "arbitrary"),
            vmem_limit_bytes=VMEM_LIMIT_BYTES,
        ),
        name="neighborhood_attn",
    )(z, z, z, bias, gq, gk)


DIL_HALF = 64
DIL_QB = 128
assert all(w // (2 * d) == DIL_HALF for w, d in DIL_GROUPS)


def _rope_tables(T):
    half = HEAD_DIM // 2
    inv = 1.0 / (ROPE_THETA ** (jnp.arange(half, dtype=jnp.float32) * 2.0 / HEAD_DIM))
    ang = jnp.arange(T, dtype=jnp.float32)[:, None] * inv[None, :]
    cos, sin = jnp.cos(ang), jnp.sin(ang)
    return jnp.concatenate([cos, cos], axis=-1), jnp.concatenate([-sin, sin], axis=-1)


def _dil_kernel(q_ref, k_ref, v_ref, qcos_ref, qsin_ref, kcos_ref, ksin_ref, qg_ref, kg_ref,
                o_ref, lse_ref, kn_ref, *, L, Lq, KW, QB, KB):
    qi = pl.program_id(3)

    def norm_rope(x, g, cos, sin):
        xn = _rms(x, g)
        return xn * cos + pltpu.roll(xn, HEAD_DIM // 2, 1) * sin

    @pl.when(qi == 0)
    def _():
        def kbody(i, carry):
            r = pl.ds(pl.multiple_of(i * KB, KB), KB)
            kn_ref[r, :] = _bf16(norm_rope(k_ref[r, :], kg_ref[...], kcos_ref[r, :], ksin_ref[r, :]))
            return carry

        lax.fori_loop(0, L // KB, kbody, 0)

    def body(i, carry):
        q0l = pl.multiple_of(i * QB, QB)
        q0 = qi * Lq + q0l
        kb = pl.multiple_of(jnp.clip(q0 - DIL_HALF, 0, L - KW), DIL_HALF)
        rows = pl.ds(q0l, QB)
        q = norm_rope(q_ref[rows, :], qg_ref[...], qcos_ref[rows, :], qsin_ref[rows, :])
        s = _dot_nt(q * (1.0 / np.sqrt(HEAD_DIM)), kn_ref[pl.ds(kb, KW), :])
        qpos = q0 + lax.broadcasted_iota(jnp.int32, (QB, KW), 0)
        kpos = kb + lax.broadcasted_iota(jnp.int32, (QB, KW), 1)
        s = jnp.where(jnp.abs(qpos - kpos) <= DIL_HALF, s, NEG_INF)
        m = jnp.max(s, axis=-1, keepdims=True)
        p = jnp.exp(s - m)
        den = jnp.sum(p, axis=-1, keepdims=True)
        o_ref[rows, :] = _dot(p, v_ref[pl.ds(kb, KW), :]) / den
        lse_ref[rows, :] = jnp.broadcast_to(m + jnp.log(den), (QB, HEAD_DIM))
        return carry

    lax.fori_loop(0, Lq // QB, body, 0, unroll=2 if Lq // QB >= 2 else 1)


C_GROUP_COLS = 3 * C_HEADS_PER_GROUP * HEAD_DIM


def _c_proj_kernel(x_ref, g_ref, w_ref, o0_ref, o1_ref, o2_ref, xn_ref, res_ref, *, tm):
    gi = pl.program_id(1)

    @pl.when(gi == 0)
    def _():
        xn_ref[...] = _bf16(_rms(x_ref[...], g_ref[...]))

    res = jnp.dot(xn_ref[...], w_ref[...], preferred_element_type=jnp.float32)
    for g, ((_, d), o_ref) in enumerate(zip(DIL_GROUPS, (o0_ref, o1_ref, o2_ref))):
        @pl.when(gi == g)
        def _(d=d, o_ref=o_ref):
            if d == 1:
                o_ref[...] = res
            else:
                for c in range(C_GROUP_COLS // LANES):
                    res_ref[c] = res[:, c * LANES:(c + 1) * LANES]
                for j in range(d):
                    for c in range(C_GROUP_COLS // LANES):
                        c0 = j * C_GROUP_COLS + c * LANES
                        o_ref[:, c0:c0 + LANES] = res_ref[c, pl.ds(j, tm // d, stride=d), :]


def _c_proj(hf, gain, w_c, *, tm=512):
    n, k = hf.shape
    hp = C_HEADS_PER_GROUP
    cols = []
    for g in range(len(DIL_GROUPS)):
        for part in range(3):
            c0 = part * C_WIDTH + g * hp * HEAD_DIM
            cols.append(w_c[:, c0:c0 + hp * HEAD_DIM])
    w_perm = jnp.concatenate(cols, axis=1).astype(jnp.bfloat16)
    return pl.pallas_call(
        functools.partial(_c_proj_kernel, tm=tm),
        out_shape=tuple(jax.ShapeDtypeStruct((n // d, d * C_GROUP_COLS), jnp.float32) for _, d in DIL_GROUPS),
        grid=(n // tm, len(DIL_GROUPS)),
        in_specs=[pl.BlockSpec((tm, k), lambda i, g: (i, 0)),
                  pl.BlockSpec((1, k), lambda i, g: (0, 0)),
                  pl.BlockSpec((k, C_GROUP_COLS), lambda i, g: (0, g))],
        out_specs=tuple(pl.BlockSpec((tm // d, d * C_GROUP_COLS), lambda i, g: (i, 0)) for _, d in DIL_GROUPS),
        scratch_shapes=[pltpu.VMEM((tm, k), jnp.bfloat16),
                        pltpu.VMEM((C_GROUP_COLS // LANES, tm, LANES), jnp.float32)],
        compiler_params=pltpu.CompilerParams(dimension_semantics=("arbitrary", "arbitrary"),
                                             vmem_limit_bytes=VMEM_LIMIT_BYTES),
        name="c_proj",
    )(hf, gain.reshape(1, k).astype(jnp.float32), w_perm)


def _dilated_group(zg, B, T, d, qnorm_g, knorm_g, cos2, sin2):
    hp = C_HEADS_PER_GROUP
    L = T // d
    Lq = min(L, 1024)
    QB = min(DIL_QB, Lq)
    KW = min(2 * DIL_QB, L)
    KB = min(L, 512)
    zv = zg.reshape(B, L, d * C_GROUP_COLS)
    cosv = cos2.reshape(L, d * HEAD_DIM)
    sinv = sin2.reshape(L, d * HEAD_DIM)
    gq = qnorm_g.reshape(1, HEAD_DIM).astype(jnp.float32)
    gk = knorm_g.reshape(1, HEAD_DIM).astype(jnp.float32)

    def zcol(part):
        return lambda b, j, h, qi: (j * 3 + part) * hp + h

    out_sds = jax.ShapeDtypeStruct((B, L, d * hp * HEAD_DIM), jnp.float32)
    out_spec = pl.BlockSpec((None, Lq, HEAD_DIM), lambda b, j, h, qi: (b, qi, j * hp + h))
    o, lse = pl.pallas_call(
        functools.partial(_dil_kernel, L=L, Lq=Lq, KW=KW, QB=QB, KB=KB),
        out_shape=(out_sds, out_sds),
        grid=(B, d, hp, L // Lq),
        in_specs=[
            pl.BlockSpec((None, Lq, HEAD_DIM), lambda b, j, h, qi: (b, qi, zcol(0)(b, j, h, qi))),
            pl.BlockSpec((None, L, HEAD_DIM), lambda b, j, h, qi: (b, 0, zcol(1)(b, j, h, qi))),
            pl.BlockSpec((None, L, HEAD_DIM), lambda b, j, h, qi: (b, 0, zcol(2)(b, j, h, qi))),
            pl.BlockSpec((Lq, HEAD_DIM), lambda b, j, h, qi: (qi, j)),
            pl.BlockSpec((Lq, HEAD_DIM), lambda b, j, h, qi: (qi, j)),
            pl.BlockSpec((L, HEAD_DIM), lambda b, j, h, qi: (0, j)),
            pl.BlockSpec((L, HEAD_DIM), lambda b, j, h, qi: (0, j)),
            pl.BlockSpec((1, HEAD_DIM), lambda b, j, h, qi: (0, 0)),
            pl.BlockSpec((1, HEAD_DIM), lambda b, j, h, qi: (0, 0)),
        ],
        out_specs=(out_spec, out_spec),
        scratch_shapes=[pltpu.VMEM((L, HEAD_DIM), jnp.bfloat16)],
        compiler_params=pltpu.CompilerParams(
            dimension_semantics=("arbitrary", "arbitrary", "arbitrary", "arbitrary"),
            vmem_limit_bytes=VMEM_LIMIT_BYTES,
        ),
        name=f"dilated_attn_d{d}",
    )(zv, zv, zv, cosv, sinv, cosv, sinv, gq, gk)
    return o.reshape(B * L, d * hp * HEAD_DIM), lse.reshape(B * L, d * hp * HEAD_DIM)


def _dilated_mixer(zgs, B, T, qnorm_g, knorm_g):
    cos2, sin2 = _rope_tables(T)
    outs, lses = [], []
    for zg, (_, d) in zip(zgs, DIL_GROUPS):
        o, lse = _dilated_group(zg, B, T, d, qnorm_g, knorm_g, cos2, sin2)
        outs.append(o)
        lses.append(lse)
    return outs, lses


def _out_proj_kernel(h_ref, a_ref, b_ref, o0_ref, o1_ref, o2_ref, l0_ref, l1_ref, l2_ref, w_ref,
                     g2_ref, wr_ref, out_ref, xn_ref, aff_ref, mix_ref, stage_ref, hrow_ref, *, tm, tn):
    gw = C_HEADS_PER_GROUP * HEAD_DIM
    n_col = D_MODEL // tn

    def token_major(src_ref, d, slot):
        if d == 1:
            return src_ref[...]
        nc = gw // LANES
        for j in range(d):
            for c in range(nc):
                c0 = j * gw + c * LANES
                stage_ref[slot * nc + c, pl.ds(j, tm // d, stride=d), :] = src_ref[:, c0:c0 + LANES]
        return jnp.concatenate([stage_ref[slot * nc + c] for c in range(nc)], axis=1)

    @pl.when(pl.program_id(1) == 0)
    def _():
        mix_ref[:, :A_WIDTH] = _bf16(a_ref[...])
        mix_ref[:, A_WIDTH:A_WIDTH + B_WIDTH] = _bf16(b_ref[...])
        dils = [d for _, d in DIL_GROUPS]
        lses = [token_major(l_ref, d, g) for g, (l_ref, d) in enumerate(zip((l0_ref, l1_ref, l2_ref), dils))]
        m = jnp.maximum(jnp.maximum(lses[0], lses[1]), lses[2])
        es = [jnp.exp(l - m) for l in lses]
        inv = 1.0 / (es[0] + es[1] + es[2])
        for g, (o_ref, d) in enumerate(zip((o0_ref, o1_ref, o2_ref), dils)):
            c0 = A_WIDTH + B_WIDTH + g * gw
            mix_ref[:, c0:c0 + gw] = _bf16(token_major(o_ref, d, len(dils) + g) * (es[g] * inv))

    j = pl.program_id(1)
    h_new = h_ref[...] + jnp.dot(mix_ref[...], w_ref[...], preferred_element_type=jnp.float32)
    out_ref[...] = h_new
    for jj in range(n_col):
        @pl.when(j == jj)
        def _(jj=jj):
            hrow_ref[:, jj * tn:(jj + 1) * tn] = h_new

    @pl.when(j == n_col - 1)
    def _():
        xn = _rms(hrow_ref[...], g2_ref[...])
        xn_ref[...] = _bf16(xn)
        logits = jnp.dot(xn, wr_ref[...], preferred_element_type=jnp.float32, precision=lax.Precision.HIGHEST)
        lane = lax.broadcasted_iota(jnp.int32, logits.shape, 1)
        logits = jnp.where(lane < N_EXPERTS, logits, NEG_INF)
        e = jnp.exp(logits - jnp.max(logits, axis=-1, keepdims=True))
        aff_ref[...] = e / jnp.sum(e, axis=-1, keepdims=True)


def _out_proj_router(hf, a_out, b_out, c_outs, c_lses, w_out, norm2_g, w_router, *, tm=512, tn=512):
    n = hf.shape[0]
    gw = C_HEADS_PER_GROUP * HEAD_DIM
    row = lambda width: pl.BlockSpec((tm, width), lambda i, j: (i, 0))
    dil = [pl.BlockSpec((tm // d, d * gw), lambda i, j: (i, 0)) for _, d in DIL_GROUPS]
    wr = jnp.zeros((D_MODEL, LANES), jnp.float32).at[:, :N_EXPERTS].set(w_router.astype(jnp.float32))
    return pl.pallas_call(
        functools.partial(_out_proj_kernel, tm=tm, tn=tn),
        out_shape=(jax.ShapeDtypeStruct((n, D_MODEL), jnp.float32),
                   jax.ShapeDtypeStruct((n, D_MODEL), jnp.bfloat16),
                   jax.ShapeDtypeStruct((n, LANES), jnp.float32)),
        grid=(n // tm, D_MODEL // tn),
        in_specs=[pl.BlockSpec((tm, tn), lambda i, j: (i, j)), row(A_WIDTH), row(B_WIDTH)]
        + dil + dil + [pl.BlockSpec((D_MODEL, tn), lambda i, j: (0, j)),
                       pl.BlockSpec((1, D_MODEL), lambda i, j: (0, 0)),
                       pl.BlockSpec((D_MODEL, LANES), lambda i, j: (0, 0))],
        out_specs=(pl.BlockSpec((tm, tn), lambda i, j: (i, j)), row(D_MODEL), row(LANES)),
        scratch_shapes=[pltpu.VMEM((tm, D_MODEL), jnp.bfloat16),
                        pltpu.VMEM((2 * len(DIL_GROUPS) * gw // LANES, tm, LANES), jnp.float32),
                        pltpu.VMEM((tm, D_MODEL), jnp.float32)],
        compiler_params=pltpu.CompilerParams(
            dimension_semantics=("arbitrary", "arbitrary"),
            vmem_limit_bytes=VMEM_LIMIT_BYTES,
        ),
        name="out_proj_router",
    )(hf, a_out.reshape(n, A_WIDTH), b_out.reshape(n, B_WIDTH), *c_outs, *c_lses,
      w_out.astype(jnp.bfloat16), norm2_g.reshape(1, D_MODEL).astype(jnp.float32), wr)


LANES = 128
FFN_ROWS = 1024
HALF_D = D_MODEL // 2
SLAB_SUB = HALF_D // LANES
COMBINE_ROWS = 512


def _pack_bf16_pairs(lo, hi):
    bits = lambda v: lax.bitcast_convert_type(_bf16(v).astype(jnp.float32), jnp.uint32)
    return (bits(lo) >> 16) | (bits(hi) & jnp.uint32(0xFFFF0000))


def _unpack_bf16_pairs(words):
    return (lax.bitcast_convert_type(words << 16, jnp.float32),
            lax.bitcast_convert_type(words & jnp.uint32(0xFFFF0000), jnp.float32))


def _expert_kernel(dest_hbm, x_ref, g_ref, wg_ref, wu_ref, wd_ref, y_hbm, ybuf, dbuf, ysem, dsem,
                   *, n_steps, steps_per_expert):
    step = pl.program_id(0) * steps_per_expert + pl.program_id(1)
    slot = step % 2

    def dest_copy(s, sl):
        return pltpu.make_async_copy(dest_hbm.at[s], dbuf.at[sl], dsem.at[sl])

    def scatter_wait(sl):
        pltpu.make_async_copy(ybuf.at[sl], ybuf.at[sl], ysem.at[sl]).wait()

    @pl.when(step == 0)
    def _():
        dest_copy(0, 0).start()

    @pl.when(step + 1 < n_steps)
    def _():
        dest_copy(step + 1, 1 - slot).start()

    @pl.when(step >= 2)
    def _():
        scatter_wait(slot)

    x = x_ref[...]
    gate = jnp.dot(x, wg_ref[...], preferred_element_type=jnp.float32)
    hdn = gate * _sigmoid(gate) * jnp.dot(x, wu_ref[...], preferred_element_type=jnp.float32)
    y = jnp.dot(_bf16(hdn), wd_ref[...], preferred_element_type=jnp.float32)
    y = y * jnp.concatenate([g_ref[...]] * (D_MODEL // LANES), axis=1)
    words = _pack_bf16_pairs(y[:, :HALF_D], y[:, HALF_D:])
    for sub in range(SLAB_SUB):
        ybuf[slot, pl.ds(sub, FFN_ROWS, stride=SLAB_SUB), :] = words[:, sub * LANES:(sub + 1) * LANES]

    dest_copy(step, slot).wait()
    for r in range(FFN_ROWS):
        pltpu.make_async_copy(ybuf.at[slot, pl.ds(r * SLAB_SUB, SLAB_SUB), :],
                              y_hbm.at[pl.ds(pl.multiple_of(dbuf[slot, r], SLAB_SUB), SLAB_SUB), :],
                              ysem.at[slot]).start()

    @pl.when(step == n_steps - 1)
    def _():
        scatter_wait(slot)
        if n_steps >= 2:
            scatter_wait(1 - slot)


def _expert_ffn(xe, gates_rep, dest, w_gate, w_up, w_down, n_tokens):
    rows, d = xe.shape
    cap = rows // N_EXPERTS
    f = w_gate.shape[-1]
    spe = cap // FFN_ROWS
    n_steps = N_EXPERTS * spe
    return pl.pallas_call(
        functools.partial(_expert_kernel, n_steps=n_steps, steps_per_expert=spe),
        out_shape=jax.ShapeDtypeStruct((N_EXPERTS * n_tokens * SLAB_SUB, LANES), jnp.uint32),
        grid=(N_EXPERTS, spe),
        in_specs=[
            pl.BlockSpec(memory_space=pl.ANY),
            pl.BlockSpec((FFN_ROWS, d), lambda e, t: (e * spe + t, 0)),
            pl.BlockSpec((FFN_ROWS, LANES), lambda e, t: (e * spe + t, 0)),
            pl.BlockSpec((None, d, f), lambda e, t: (e, 0, 0), pipeline_mode=pl.Buffered(1)),
            pl.BlockSpec((None, d, f), lambda e, t: (e, 0, 0), pipeline_mode=pl.Buffered(1)),
            pl.BlockSpec((None, f, d), lambda e, t: (e, 0, 0), pipeline_mode=pl.Buffered(1)),
        ],
        out_specs=pl.BlockSpec(memory_space=pl.ANY),
        scratch_shapes=[pltpu.VMEM((2, FFN_ROWS * SLAB_SUB, LANES), jnp.uint32),
                        pltpu.SMEM((2, FFN_ROWS), jnp.int32),
                        pltpu.SemaphoreType.DMA((2,)),
                        pltpu.SemaphoreType.DMA((2,))],
        compiler_params=pltpu.CompilerParams(dimension_semantics=("arbitrary", "arbitrary"),
                                             vmem_limit_bytes=VMEM_LIMIT_BYTES),
        name="expert_ffn",
    )((dest * SLAB_SUB).reshape(n_steps, FFN_ROWS), xe, gates_rep, w_gate.astype(jnp.bfloat16),
      w_up.astype(jnp.bfloat16), w_down.astype(jnp.bfloat16))


COMBINE_RANKS = 4
COMBINE_STEPS = N_EXPERTS // COMBINE_RANKS
PLE_TN = 512
PLE_STEPS = D_MODEL // PLE_TN


def _combine_ple_kernel(tmax_ref, h_ref, cnt_ref, *refs):
    y_refs = refs[:COMBINE_RANKS]
    (p_ref, gg_ref, pg_ref, wp_ref, wg_ref, out_ref, h2_ref, xn_ref, e_ref, acc_ref, cnt8_ref) = refs[COMBINE_RANKS:]
    i, step = pl.program_id(0), pl.program_id(1)
    subs_per_tile = PLE_TN // LANES

    @pl.when(step == 0)
    def _():
        acc_ref[...] = jnp.zeros_like(acc_ref)
        for sub in range(SLAB_SUB):
            cnt8_ref[pl.ds(sub, COMBINE_ROWS, stride=SLAB_SUB), :] = cnt_ref[...]

    for q, y_ref in enumerate(y_refs):
        @pl.when((step < COMBINE_STEPS) & (step * COMBINE_RANKS + q < tmax_ref[i]))
        def _(q=q, y_ref=y_ref):
            chosen = cnt8_ref[...] > step * COMBINE_RANKS + q
            for hh, half in enumerate(_unpack_bf16_pairs(y_ref[...])):
                acc_ref[hh] += jnp.where(chosen, half, 0.0)

    @pl.when(step == COMBINE_STEPS)
    def _():
        for sub in range(SLAB_SUB):
            cols = slice((sub % subs_per_tile) * LANES, (sub % subs_per_tile + 1) * LANES)
            for hh in range(2):
                tile = hh * (HALF_D // PLE_TN) + sub // subs_per_tile
                c0 = tile * PLE_TN + (sub % subs_per_tile) * LANES
                h2_ref[tile, :, cols] = (h_ref[:, c0:c0 + LANES]
                                         + acc_ref[hh, pl.ds(sub, COMBINE_ROWS, stride=SLAB_SUB), :])
        h2 = jnp.concatenate([h2_ref[jj] for jj in range(PLE_STEPS)], axis=1)
        xn_ref[...] = _bf16(_rms(h2, gg_ref[...]))
        e = _rms(jnp.dot(_bf16(p_ref[...]), wp_ref[...], preferred_element_type=jnp.float32), pg_ref[...])
        for jj in range(PLE_STEPS):
            e_ref[jj] = e[:, jj * PLE_TN:(jj + 1) * PLE_TN]

    @pl.when(step >= COMBINE_STEPS)
    def _():
        j = step - COMBINE_STEPS
        gate = _sigmoid(jnp.dot(xn_ref[...], wg_ref[...], preferred_element_type=jnp.float32))
        out_ref[...] = h2_ref[j] + gate * e_ref[j]


def _combine_ple(hf, slabs, cnt_rep, tile_max, p, w_ple, ple_norm_g, gate_norm_g, w_ple_gate):
    n, d = hf.shape
    tm = COMBINE_ROWS
    pd = p.shape[1]
    slabs = slabs.reshape(N_EXPERTS, n * SLAB_SUB, LANES)
    vec = lambda g: g.reshape(1, D_MODEL).astype(jnp.float32)

    def slab_spec(q):
        return pl.BlockSpec((None, tm * SLAB_SUB, LANES), lambda i, s, tmax: (
            jnp.minimum(jnp.minimum(s, COMBINE_STEPS - 1) * COMBINE_RANKS + q, jnp.maximum(tmax[i], 1) - 1), i, 0))

    col = lambda i, s, tmax: jnp.maximum(s - COMBINE_STEPS, 0)
    return pl.pallas_call(
        _combine_ple_kernel,
        out_shape=jax.ShapeDtypeStruct((n, d), jnp.float32),
        grid_spec=pltpu.PrefetchScalarGridSpec(
            num_scalar_prefetch=1,
            grid=(n // tm, COMBINE_STEPS + PLE_STEPS),
            in_specs=[pl.BlockSpec((tm, d), lambda i, s, tmax: (i, 0)),
                      pl.BlockSpec((tm, LANES), lambda i, s, tmax: (i, 0))]
            + [slab_spec(q) for q in range(COMBINE_RANKS)]
            + [pl.BlockSpec((tm, pd), lambda i, s, tmax: (i, 0)),
               pl.BlockSpec((1, d), lambda i, s, tmax: (0, 0)),
               pl.BlockSpec((1, d), lambda i, s, tmax: (0, 0)),
               pl.BlockSpec((pd, d), lambda i, s, tmax: (0, 0)),
               pl.BlockSpec((d, PLE_TN), lambda i, s, tmax: (0, col(i, s, tmax)))],
            out_specs=pl.BlockSpec((tm, PLE_TN), lambda i, s, tmax: (i, col(i, s, tmax))),
            scratch_shapes=[pltpu.VMEM((PLE_STEPS, tm, PLE_TN), jnp.float32),
                            pltpu.VMEM((tm, d), jnp.bfloat16),
                            pltpu.VMEM((PLE_STEPS, tm, PLE_TN), jnp.float32),
                            pltpu.VMEM((2, tm * SLAB_SUB, LANES), jnp.float32),
                            pltpu.VMEM((tm * SLAB_SUB, LANES), jnp.int32)],
        ),
        compiler_params=pltpu.CompilerParams(dimension_semantics=("arbitrary", "arbitrary"),
                                             vmem_limit_bytes=VMEM_LIMIT_BYTES),
        name="ffn_combine_ple",
    )(tile_max, hf, cnt_rep, *([slabs] * COMBINE_RANKS), p, vec(gate_norm_g), vec(ple_norm_g),
      w_ple.astype(jnp.bfloat16), w_ple_gate.astype(jnp.bfloat16))


def _expert_choice_slabs(xn, aff, w_gate, w_up, w_down):
    n = xn.shape[0]
    cap = (EC_CAPACITY_FACTOR * n) // N_EXPERTS
    gates, idx = lax.top_k(aff[:, :N_EXPERTS].T, cap)
    xe = xn[idx.reshape(-1)]
    chosen = jnp.zeros((N_EXPERTS, n), jnp.int32).at[jnp.arange(N_EXPERTS)[:, None], idx].set(1)
    rank = jnp.cumsum(chosen, axis=0) - chosen
    dest = jnp.take_along_axis(rank, idx, axis=1) * n + idx
    cnt = jnp.sum(chosen, axis=0)
    slabs = _expert_ffn(xe, jnp.broadcast_to(gates.reshape(-1, 1), (N_EXPERTS * cap, LANES)),
                        dest.reshape(-1).astype(jnp.int32), w_gate, w_up, w_down, n)
    cnt_rep = jnp.broadcast_to(cnt[:, None], (n, LANES)).astype(jnp.int32)
    tile_max = jnp.max(cnt.reshape(n // COMBINE_ROWS, COMBINE_ROWS), axis=1).astype(jnp.int32)
    return slabs, cnt_rep, tile_max


def _layer(h, p_i, lb_f, lb_b, norm1_g, w_in, a_norm_g, b_qnorm_g, b_knorm_g, b_rpb,
           c_qnorm_g, c_knorm_g, w_out, norm2_g, w_router, w_gate, w_up, w_down,
           w_ple, ple_norm_g, gate_norm_g, w_ple_gate):
    B, T, _ = h.shape
    N = B * T
    hf = h.reshape(N, D_MODEL)
    ab_width = IN_WIDTH - 3 * C_WIDTH
    z_ab = _matmul(hf, w_in[:, :ab_width], norm1_g, tn=768).reshape(B, T, ab_width)
    z_c = _c_proj(hf, norm1_g, w_in[:, ab_width:])
    a_out = _hgrn2_mixer(z_ab, lb_f, lb_b, a_norm_g)
    b_out = _neighborhood_mixer(z_ab, b_rpb, b_qnorm_g, b_knorm_g)
    c_outs, c_lses = _dilated_mixer(z_c, B, T, c_qnorm_g, c_knorm_g)
    hf, xn, aff = _out_proj_router(hf, a_out, b_out, c_outs, c_lses, w_out, norm2_g, w_router)
    slabs, cnt_rep, tile_max = _expert_choice_slabs(xn, aff, w_gate, w_up, w_down)
    out = _combine_ple(hf, slabs, cnt_rep, tile_max, p_i.reshape(N, -1), w_ple, ple_norm_g, gate_norm_g,
                       w_ple_gate)
    return out.reshape(B, T, D_MODEL)


def _trunk(h, p, lb, weights):
    (norm1_g, w_in, a_norm_g, b_qnorm_g, b_knorm_g, b_rpb, c_qnorm_g, c_knorm_g, w_out,
     norm2_g, w_router, w_gate, w_up, w_down, w_ple, ple_norm_g, gate_norm_g, w_ple_gate) = weights
    for i in range(DEPTH):
        h = _layer(h, p[i], lb[0, i], lb[1, i], norm1_g[i], w_in[i], a_norm_g[i], b_qnorm_g[i],
                   b_knorm_g[i], b_rpb[i], c_qnorm_g[i], c_knorm_g[i], w_out[i], norm2_g[i],
                   w_router[i], w_gate[i], w_up[i], w_down[i], w_ple[i], ple_norm_g[i],
                   gate_norm_g[i], w_ple_gate[i])
    return h


def kernel(x_prompt, x_sample, p_prompt, p_sample, norm1_g, w_in, lb_logits, a_norm_g,
           b_qnorm_g, b_knorm_g, b_rpb, c_qnorm_g, c_knorm_g, w_out, norm2_g, w_router,
           w_gate, w_up, w_down, w_ple, ple_norm_g, gate_norm_g, w_ple_gate):
    pr = jax.nn.softmax(lb_logits.astype(jnp.float32), axis=1)
    lb = jnp.cumsum(pr, axis=1) - pr[:, :1]
    weights = (norm1_g, w_in, a_norm_g, b_qnorm_g, b_knorm_g, b_rpb, c_qnorm_g, c_knorm_g, w_out,
               norm2_g, w_router, w_gate, w_up, w_down, w_ple, ple_norm_g, gate_norm_g, w_ple_gate)
    y_prompt = _trunk(x_prompt, p_prompt, lb, weights)
    y_sample = _trunk(x_sample, p_sample, lb, weights)
    return (y_prompt, y_sample)
```
